```python
import math
import jax, jax.numpy as jnp
from jax import lax
import numpy as np

D_MODEL = 1024
BATCH = 8
SEQ = 4096
DEPTH = 4

N_Q_HEADS = 8
N_KV_HEADS = 2
HEAD_DIM = 64
ATTN_WIDTH = N_Q_HEADS * HEAD_DIM
KV_WIDTH = N_KV_HEADS * HEAD_DIM
Q_BLOCK = 128
ROPE_THETA = 10000.0
ROPE_AXIS_DIM = HEAD_DIM // 2
GRID_W = 64
SSM_WIDTH = D_MODEL // 2
SSM_GROUP = 16
SSM_GROUPS = SSM_WIDTH // SSM_GROUP
SSM_STATE = 64
N_DIRS = 2
DT_MIN = 1e-3
DT_MAX = 1e-1
LAMBDA_RE_MAX = -1e-4
MIX_WIDTH = ATTN_WIDTH + SSM_WIDTH
IN_WIDTH = ATTN_WIDTH + 2 * KV_WIDTH + SSM_WIDTH
FFN_HIDDEN = -(-8 * D_MODEL // (3 * 256)) * 256
N_MOD = 6
NORM_EPS = 1e-6

kernel_name = "hymba_style_gqa_s5_bidir_encoder"


def rms_norm(x, gain):
    xf = x.astype(jnp.float32)
    y = xf * lax.rsqrt(jnp.mean(xf * xf, axis=-1, keepdims=True) + NORM_EPS)
    return (y * gain.astype(jnp.float32)).astype(x.dtype)


def axial_rope_tables(seq_len):
    rows = seq_len // GRID_W
    row_idx = jnp.repeat(jnp.arange(rows, dtype=jnp.float32), GRID_W)
    col_idx = jnp.tile(jnp.arange(GRID_W, dtype=jnp.float32), rows)
    inv_freq = 1.0 / (ROPE_THETA ** (jnp.arange(0, ROPE_AXIS_DIM, 2, dtype=jnp.float32) / ROPE_AXIS_DIM))
    ang = jnp.concatenate([row_idx[:, None] * inv_freq, col_idx[:, None] * inv_freq], axis=-1)
    return jnp.cos(ang), jnp.sin(ang)


def apply_rope(x, cos, sin):
    xf = x.astype(jnp.float32).reshape(*x.shape[:-1], HEAD_DIM // 2, 2)
    x1, x2 = xf[..., 0], xf[..., 1]
    cs = cos[None, :, None, :]
    sn = sin[None, :, None, :]
    out = jnp.stack([x1 * cs - x2 * sn, x1 * sn + x2 * cs], axis=-1).reshape(x.shape)
    return out.astype(x.dtype)


def block_attention(q, k, v):
    bsz, seq_len = q.shape[0], q.shape[1]
    n_blocks = seq_len // Q_BLOCK
    rep = N_Q_HEADS // N_KV_HEADS
    scale = HEAD_DIM ** -0.5
    qb = q.reshape(bsz, n_blocks, Q_BLOCK, N_KV_HEADS, rep, HEAD_DIM).transpose(1, 0, 2, 3, 4, 5)

    def one_block(q_blk):
        s = jnp.einsum('bqgrd,bkgd->bgrqk', q_blk, k, preferred_element_type=jnp.float32) * scale
        p = jax.nn.softmax(s, axis=-1)
        return jnp.einsum('bgrqk,bkgd->bqgrd', p.astype(v.dtype), v)

    o = lax.map(one_block, qb)
    return o.transpose(1, 0, 2, 3, 4, 5).reshape(bsz, seq_len, ATTN_WIDTH)


def _ssm_combine(e1, e2):
    a1r, a1i, b1r, b1i = e1
    a2r, a2i, b2r, b2i = e2
    return (a2r * a1r - a2i * a1i,
            a2r * a1i + a2i * a1r,
            a2r * b1r - a2i * b1i + b2r,
            a2r * b1i + a2i * b1r + b2i)


def ssm_direction(u, lam_re, lam_im, log_dt, b_re, b_im, c_re, c_im, reverse):
    lr = jnp.minimum(lam_re.astype(jnp.float32), LAMBDA_RE_MAX)
    li = lam_im.astype(jnp.float32)
    dt = jnp.exp(log_dt.astype(jnp.float32))[:, None]
    mag = jnp.exp(lr * dt)
    a_re = mag * jnp.cos(li * dt)
    a_im = mag * jnp.sin(li * dt)
    den = lr * lr + li * li
    n_re = a_re - 1.0
    k_re = (n_re * lr + a_im * li) / den
    k_im = (a_im * lr - n_re * li) / den
    br = b_re.astype(jnp.float32)
    bi = b_im.astype(jnp.float32)
    bb_re = k_re[..., None] * br - k_im[..., None] * bi
    bb_im = k_re[..., None] * bi + k_im[..., None] * br
    x_re = jnp.einsum('gpc,blgc->blgp', bb_re, u)
    x_im = jnp.einsum('gpc,blgc->blgp', bb_im, u)
    a_re_b = jnp.broadcast_to(a_re, x_re.shape)
    a_im_b = jnp.broadcast_to(a_im, x_im.shape)
    _, _, h_re, h_im = lax.associative_scan(_ssm_combine, (a_re_b, a_im_b, x_re, x_im),
                                            reverse=reverse, axis=1)
    return (jnp.einsum('gcp,blgp->blgc', c_re.astype(jnp.float32), h_re)
            - jnp.einsum('gcp,blgp->blgc', c_im.astype(jnp.float32), h_im))


def ssm_mixer(u, lam_re, lam_im, log_dt, b_re, b_im, c_re, c_im, d_skip, w_glu, b_glu):
    bsz, seq_len = u.shape[0], u.shape[1]
    uf = u.astype(jnp.float32).reshape(bsz, seq_len, SSM_GROUPS, SSM_GROUP)
    y = ssm_direction(uf, lam_re[0], lam_im[0], log_dt[0], b_re[0], b_im[0], c_re[0], c_im[0], False)
    y = y + ssm_direction(uf, lam_re[1], lam_im[1], log_dt[1], b_re[1], b_im[1], c_re[1], c_im[1], True)
    y = y.reshape(bsz, seq_len, SSM_WIDTH) + d_skip.astype(jnp.float32) * uf.reshape(bsz, seq_len, SSM_WIDTH)
    y = jax.nn.gelu(y).astype(u.dtype)
    return y * jax.nn.sigmoid(y @ w_glu + b_glu)


def setup_inputs(seed: int = 0) -> dict:
    key = jax.random.key(seed)
    ks = jax.random.split(key, 24)
    f32 = jnp.float32
    nrm = lambda k, shape, s: jax.random.normal(k, shape, f32) * s
    G, P, C = SSM_GROUPS, SSM_STATE, SSM_GROUP
    lam_im_base = math.pi * jnp.arange(P, dtype=f32)
    return {
        "x": nrm(ks[0], (BATCH, SEQ, D_MODEL), 1.0),
        "c": nrm(ks[1], (BATCH, D_MODEL), 1.0),
        "w_ada": nrm(ks[2], (DEPTH, D_MODEL, N_MOD * D_MODEL), 0.5 * D_MODEL ** -0.5),
        "b_ada": nrm(ks[3], (DEPTH, N_MOD * D_MODEL), 0.01),
        "norm1": 1.0 + nrm(ks[4], (DEPTH, D_MODEL), 0.02),
        "w_in": nrm(ks[5], (DEPTH, D_MODEL, IN_WIDTH), D_MODEL ** -0.5),
        "q_norm": 1.0 + nrm(ks[6], (DEPTH, HEAD_DIM), 0.02),
        "k_norm": 1.0 + nrm(ks[7], (DEPTH, HEAD_DIM), 0.02),
        "ssm_lam_re": -0.5 + nrm(ks[8], (DEPTH, N_DIRS, G, P), 0.01),
        "ssm_lam_im": lam_im_base + nrm(ks[9], (DEPTH, N_DIRS, G, P), 0.01),
        "ssm_log_dt": jax.random.uniform(ks[10], (DEPTH, N_DIRS, G), f32, math.log(DT_MIN), math.log(DT_MAX)),
        "ssm_b_re": nrm(ks[11], (DEPTH, N_DIRS, G, P, C), (2.0 * C) ** -0.5),
        "ssm_b_im": nrm(ks[12], (DEPTH, N_DIRS, G, P, C), (2.0 * C) ** -0.5),
        "ssm_c_re": nrm(ks[13], (DEPTH, N_DIRS, G, C, P), (2.0 * P) ** -0.5),
        "ssm_c_im": nrm(ks[14], (DEPTH, N_DIRS, G, C, P), (2.0 * P) ** -0.5),
        "ssm_d": nrm(ks[15], (DEPTH, SSM_WIDTH), 1.0),
        "w_glu": nrm(ks[16], (DEPTH, SSM_WIDTH, SSM_WIDTH), SSM_WIDTH ** -0.5),
        "b_glu": nrm(ks[17], (DEPTH, SSM_WIDTH), 0.01),
        "attn_out_norm": 1.0 + nrm(ks[18], (DEPTH, ATTN_WIDTH), 0.02),
        "ssm_out_norm": 1.0 + nrm(ks[19], (DEPTH, SSM_WIDTH), 0.02),
        "w_out": nrm(ks[20], (DEPTH, MIX_WIDTH, D_MODEL), MIX_WIDTH ** -0.5),
        "norm2": 1.0 + nrm(ks[21], (DEPTH, D_MODEL), 0.02),
        "w_ffn_in": nrm(ks[22], (DEPTH, D_MODEL, 2 * FFN_HIDDEN), D_MODEL ** -0.5),
        "w_ffn_out": nrm(ks[23], (DEPTH, FFN_HIDDEN, D_MODEL), FFN_HIDDEN ** -0.5),
        "final_norm": 1.0 + nrm(jax.random.fold_in(key, 99), (D_MODEL,), 0.02),
    }


def reference(x, c, w_ada, b_ada, norm1, w_in, q_norm, k_norm, ssm_lam_re, ssm_lam_im, ssm_log_dt,
              ssm_b_re, ssm_b_im, ssm_c_re, ssm_c_im, ssm_d, w_glu, b_glu, attn_out_norm, ssm_out_norm,
              w_out, norm2, w_ffn_in, w_ffn_out, final_norm):
    bsz, seq_len = x.shape[0], x.shape[1]
    cos, sin = axial_rope_tables(seq_len)
    c_act = jax.nn.silu(c)
    split_pts = [ATTN_WIDTH, ATTN_WIDTH + KV_WIDTH, ATTN_WIDTH + 2 * KV_WIDTH]
    for i in range(DEPTH):
        mod = (c_act @ w_ada[i] + b_ada[i])[:, None, :]
        sh1, sc1, g1, sh2, sc2, g2 = jnp.split(mod, N_MOD, axis=-1)
        h = rms_norm(x, norm1[i]) * (1.0 + sc1) + sh1
        z = h @ w_in[i]
        q, k, v, u = jnp.split(z, split_pts, axis=-1)
        q = apply_rope(rms_norm(q.reshape(bsz, seq_len, N_Q_HEADS, HEAD_DIM), q_norm[i]), cos, sin)
        k = apply_rope(rms_norm(k.reshape(bsz, seq_len, N_KV_HEADS, HEAD_DIM), k_norm[i]), cos, sin)
        v = v.reshape(bsz, seq_len, N_KV_HEADS, HEAD_DIM)
        o_attn = block_attention(q, k, v)
        o_ssm = ssm_mixer(u, ssm_lam_re[i], ssm_lam_im[i], ssm_log_dt[i], ssm_b_re[i], ssm_b_im[i],
                          ssm_c_re[i], ssm_c_im[i], ssm_d[i], w_glu[i], b_glu[i])
        mix = jnp.concatenate([rms_norm(o_attn, attn_out_norm[i]), rms_norm(o_ssm, ssm_out_norm[i])], axis=-1)
        x = x + g1 * (mix @ w_out[i])
        h = rms_norm(x, norm2[i]) * (1.0 + sc2) + sh2
        gt, up = jnp.split(h @ w_ffn_in[i], 2, axis=-1)
        x = x + g2 * ((jax.nn.silu(gt) * up) @ w_ffn_out[i])
    return rms_norm(x, final_norm)
```

```python
import functools

import jax
import jax.numpy as jnp
from jax import lax
from jax.experimental import pallas as pl
from jax.experimental.pallas import tpu as pltpu

F32 = jnp.float32
BF16 = jnp.bfloat16

N_Q_HEADS = 8
N_KV_HEADS = 2
HEAD_DIM = 64
HALF_HEAD = HEAD_DIM // 2
GRID_W = 64
ROPE_THETA = 10000.0
SSM_GROUP = 16
SSM_STATE = 64
LAMBDA_RE_MAX = -1e-4
NORM_EPS = 1e-6
N_MOD = 6

LANES = 128
SUBLANES = 8
SSM_CHUNK = 16
TOKEN_TILE = 512
FFN_TOKEN_TILE = 256
Q_TILE = 128
SSM_GROUPS_PER_STEP = 4
PREP_GROUPS_PER_STEP = 8
VMEM_LIMIT = 56 * 1024 * 1024

_DN_NT = (((1,), (1,)), ((), ()))


def _cparams(*sem):
    return pltpu.CompilerParams(dimension_semantics=sem, vmem_limit_bytes=VMEM_LIMIT)


def _rms(x):
    return x * lax.rsqrt(jnp.mean(x * x, axis=-1, keepdims=True) + NORM_EPS)


def _mod_kernel(c_ref, w_ref, b_ref, o_ref):
    c = c_ref[...]
    ca = (c * jax.nn.sigmoid(c)).astype(BF16)
    o_ref[0] = jnp.dot(ca, w_ref[0].astype(BF16), preferred_element_type=F32) + b_ref[0]


def _modulation(c, w_ada, b_ada):
    depth, d, n = w_ada.shape
    bsz = c.shape[0]
    tn = 1536
    return pl.pallas_call(
        _mod_kernel,
        grid=(depth, n // tn),
        in_specs=[
            pl.BlockSpec((bsz, d), lambda i, j: (0, 0)),
            pl.BlockSpec((1, d, tn), lambda i, j: (i, 0, j)),
            pl.BlockSpec((1, 1, tn), lambda i, j: (i, 0, j)),
        ],
        out_specs=pl.BlockSpec((1, bsz, tn), lambda i, j: (i, 0, j)),
        out_shape=jax.ShapeDtypeStruct((depth, bsz, n), F32),
        compiler_params=_cparams("parallel", "parallel"),
        name="adaln_mod",
    )(c, w_ada, b_ada.reshape(depth, 1, n))


def _norm_rope(zt, gain, cos, sin, ones_blk):
    ssq = jnp.dot((zt * zt).astype(BF16), ones_blk, preferred_element_type=F32)
    y = zt * lax.rsqrt(ssq * (1.0 / HEAD_DIM) + NORM_EPS) * gain
    lane = lax.broadcasted_iota(jnp.int32, y.shape, 1)
    partner = jnp.where((lane & HALF_HEAD) == 0,
                        pltpu.roll(y, LANES - HALF_HEAD, 1), pltpu.roll(y, HALF_HEAD, 1))
    return y * cos + partner * sin


def _premix_kernel(x_ref, mod_ref, n1_ref, w_ref, qg_ref, kg_ref, cos_ref, sin_ref, ones_ref,
                   q_ref, k_ref, v_ref, u_ref):
    x = x_ref[0]
    m = mod_ref[0, 0]
    h = _rms(x) * n1_ref[...]
    h = h * (1.0 + m[1:2]) + m[0:1]
    z = jnp.dot(h.astype(BF16), w_ref[...], preferred_element_type=F32)
    cos = cos_ref[...]
    sin = sin_ref[...]
    ones_blk = ones_ref[...]
    qg = qg_ref[...] * (HEAD_DIM ** -0.5)
    nq = N_Q_HEADS * LANES
    for hd in range(N_Q_HEADS):
        zt = z[:, hd * LANES:(hd + 1) * LANES]
        q_ref[0, :, hd * LANES:(hd + 1) * LANES] = _norm_rope(zt, qg, cos, sin, ones_blk).astype(BF16)
    k_ref[0] = _norm_rope(z[:, nq:nq + LANES], kg_ref[...], cos, sin, ones_blk).astype(BF16)
    v = z[:, nq + LANES:nq + 2 * LANES]
    v_ref[0, :, 0:LANES] = v.astype(BF16)
    v_ref[0, :, LANES:2 * LANES] = jnp.ones_like(v).astype(BF16)
    u_ref[0] = z[:, nq + 2 * LANES:]


def _premix(x, mod6, layer, n1, w_exp, qg, kg, cos_t, sin_t, ones_blk):
    bsz, seq, d = x.shape
    nc = w_exp.shape[1]
    nq = N_Q_HEADS * LANES
    nu = nc - nq - 2 * LANES
    tm = min(TOKEN_TILE, seq)
    row = lambda b, i: (b, i, 0)
    const = lambda b, i: (0, 0)
    return pl.pallas_call(
        _premix_kernel,
        grid=(bsz, seq // tm),
        in_specs=[
            pl.BlockSpec((1, tm, d), row),
            pl.BlockSpec((1, 1, N_MOD, d), lambda b, i: (layer, b, 0, 0)),
            pl.BlockSpec((1, d), const),
            pl.BlockSpec((d, nc), const),
            pl.BlockSpec((1, LANES), const),
            pl.BlockSpec((1, LANES), const),
            pl.BlockSpec((tm, LANES), lambda b, i: (i, 0)),
            pl.BlockSpec((tm, LANES), lambda b, i: (i, 0)),
            pl.BlockSpec((LANES, LANES), const),
        ],
        out_specs=[
            pl.BlockSpec((1, tm, nq), row),
            pl.BlockSpec((1, tm, LANES), row),
            pl.BlockSpec((1, tm, 2 * LANES), row),
            pl.BlockSpec((1, tm, nu), row),
        ],
        out_shape=[
            jax.ShapeDtypeStruct((bsz, seq, nq), BF16),
            jax.ShapeDtypeStruct((bsz, seq, LANES), BF16),
            jax.ShapeDtypeStruct((bsz, seq, 2 * LANES), BF16),
            jax.ShapeDtypeStruct((bsz, seq, nu), F32),
        ],
        compiler_params=_cparams("parallel", "parallel"),
        name="premix",
    )(x, mod6, n1, w_exp, qg, kg, cos_t, sin_t, ones_blk)


def _attn_kernel(q_ref, k_ref, v_ref, o_ref):
    k = k_ref[0]
    v = v_ref[0]
    rep = N_Q_HEADS // N_KV_HEADS
    tq = q_ref.shape[1]
    for g in range(N_KV_HEADS):
        qs = jnp.concatenate(
            [q_ref[0, :, (g * rep + r) * LANES:(g * rep + r + 1) * LANES] for r in range(rep)], axis=0)
        s = lax.dot_general(qs, k, _DN_NT, preferred_element_type=F32)
        e = jnp.exp(s - jnp.max(s, axis=-1, keepdims=True)).astype(BF16)
        pv = jnp.dot(e, v, preferred_element_type=F32)
        o = pv[:, 0:LANES] / pv[:, LANES:2 * LANES]
        lane = lax.broadcasted_iota(jnp.int32, (tq, LANES), 1)
        for j in range(rep // 2):
            even = o[(2 * j) * tq:(2 * j + 1) * tq]
            odd = o[(2 * j + 1) * tq:(2 * j + 2) * tq]
            if g == 0:
                tile = jnp.where(lane < HEAD_DIM, even, pltpu.roll(odd, HEAD_DIM, 1))
            else:
                tile = jnp.where(lane < HEAD_DIM, pltpu.roll(even, HEAD_DIM, 1), odd)
            t = g * (rep // 2) + j
            o_ref[0, :, t * LANES:(t + 1) * LANES] = tile


def _attention(q, k, v):
    bsz, seq, nq = q.shape
    tq = min(Q_TILE, seq)
    width = N_Q_HEADS * HEAD_DIM
    return pl.pallas_call(
        _attn_kernel,
        grid=(bsz, seq // tq),
        in_specs=[
            pl.BlockSpec((1, tq, nq), lambda b, i: (b, i, 0)),
            pl.BlockSpec((1, seq, LANES), lambda b, i: (b, 0, 0)),
            pl.BlockSpec((1, seq, 2 * LANES), lambda b, i: (b, 0, 0)),
        ],
        out_specs=pl.BlockSpec((1, tq, width), lambda b, i: (b, i, 0)),
        out_shape=jax.ShapeDtypeStruct((bsz, seq, width), F32),
        compiler_params=_cparams("parallel", "parallel"),
        name="attention",
    )(q, k, v)


def _cmul(ar, ai, br, bi):
    return ar * br - ai * bi, ar * bi + ai * br


def _cpow_table(sq_re, sq_im, expo):
    pr = jnp.ones(expo.shape, F32)
    pi = jnp.zeros(expo.shape, F32)
    for bit, (sr, si) in enumerate(zip(sq_re, sq_im)):
        nr, ni = _cmul(pr, pi, sr, si)
        take = (expo & (1 << bit)) != 0
        pr = jnp.where(take, nr, pr)
        pi = jnp.where(take, ni, pi)
    return pr, pi


def _split_bf16(a):
    hi = a.astype(BF16)
    return hi, (a - hi.astype(F32)).astype(BF16)


def _dot_nt_3pass(a, b):
    ah, al = _split_bf16(a)
    bh, bl = _split_bf16(b)
    dg = functools.partial(lax.dot_general, dimension_numbers=_DN_NT, preferred_element_type=F32)
    return dg(ah, bh) + (dg(ah, bl) + dg(al, bh))


def _ssm_prep_kernel(lr_ref, li_ref, ldt_ref, br_ref, bi_ref, cr_ref, ci_ref,
                     w_ref, vt_ref, kf_ref, kb_ref, a_ref):
    ngrp = lr_ref.shape[1]
    t = SSM_CHUNK
    rows = t * SSM_GROUP
    shape = (rows, LANES)
    pos = lax.broadcasted_iota(jnp.int32, shape, 0) // SSM_GROUP
    lane = lax.broadcasted_iota(jnp.int32, shape, 1)
    fwd = lane < SSM_STATE
    lane_c = lax.broadcasted_iota(jnp.int32, (SSM_GROUP, LANES), 1)
    for gi in range(ngrp):
        lr = jnp.minimum(lr_ref[0, gi:gi + 1, :], LAMBDA_RE_MAX)
        li = li_ref[0, gi:gi + 1, :]
        dt = jnp.exp(ldt_ref[0, gi:gi + 1, :])
        mag = jnp.exp(lr * dt)
        a_re = mag * jnp.cos(li * dt)
        a_im = mag * jnp.sin(li * dt)
        den = lr * lr + li * li
        n_re = a_re - 1.0
        k_re = (n_re * lr + a_im * li) / den
        k_im = (a_im * lr - n_re * li) / den
        br = br_ref[0, gi]
        bi = bi_ref[0, gi]
        bb_re = k_re * br - k_im * bi
        bb_im = k_re * bi + k_im * br
        bb_re = jnp.concatenate([bb_re] * t, axis=0)
        bb_im = jnp.concatenate([bb_im] * t, axis=0)
        cr = cr_ref[0, gi]
        ci = ci_ref[0, gi]
        c_re = jnp.concatenate([cr] * t, axis=0)
        c_im = jnp.concatenate([ci] * t, axis=0)
        sq_re, sq_im = [a_re], [a_im]
        for _ in range(4):
            r2, i2 = _cmul(sq_re[-1], sq_im[-1], sq_re[-1], sq_im[-1])
            sq_re.append(r2)
            sq_im.append(i2)
        a_ref[0, gi, 0:1, :] = sq_re[4]
        a_ref[0, gi, 1:2, :] = sq_im[4]
        pr, pi = _cpow_table(sq_re[:4], sq_im[:4], jnp.where(fwd, t - 1 - pos, pos))
        wr, wi = _cmul(pr, pi, bb_re, bb_im)
        w_ref[0, gi, :, 0:LANES] = wr
        w_ref[0, gi, :, LANES:2 * LANES] = wi
        pr, pi = _cpow_table(sq_re, sq_im, jnp.where(fwd, pos + 1, t - pos))
        vr, vi = _cmul(pr, pi, c_re, c_im)
        vt_ref[0, gi, :, 0:LANES] = vr
        vt_ref[0, gi, :, LANES:2 * LANES] = -vi
        pr, pi = _cpow_table(sq_re[:4], sq_im[:4], pos)
        gr, gim = _cmul(pr, pi, bb_re, bb_im)
        zero = jnp.zeros_like(cr)
        cfr = jnp.where(lane_c < SSM_STATE, cr, zero)
        cfi = jnp.where(lane_c < SSM_STATE, ci, zero)
        cbr = jnp.where(lane_c < SSM_STATE, zero, cr)
        cbi = jnp.where(lane_c < SSM_STATE, zero, ci)
        kf = _dot_nt_3pass(gr, cfr) - _dot_nt_3pass(gim, cfi)
        kb = _dot_nt_3pass(gr, cbr) - _dot_nt_3pass(gim, cbi)
        row = lax.broadcasted_iota(jnp.int32, kf.shape, 0)
        kf_ref[0, gi] = kf + jnp.where(row < SSM_GROUP, kb, jnp.zeros_like(kb))
        kb_ref[0, gi] = kb


def _ssm_prep(lam_re, lam_im, log_dt, b_re, b_im, c_re, c_im):
    depth, _, ngroups, nstate = lam_re.shape
    cat = lambda a: jnp.concatenate([a[:, 0], a[:, 1]], axis=-1)
    lr = cat(lam_re)
    li = cat(lam_im)
    ldt = cat(jnp.broadcast_to(log_dt[..., None], lam_re.shape))
    bt_re = cat(jnp.swapaxes(b_re, -1, -2))
    bt_im = cat(jnp.swapaxes(b_im, -1, -2))
    cc_re = cat(c_re)
    cc_im = cat(c_im)
    gb = min(PREP_GROUPS_PER_STEP, ngroups)
    rows = SSM_CHUNK * SSM_GROUP
    vec = pl.BlockSpec((1, gb, 2 * nstate), lambda i, j: (i, j, 0))
    mat = pl.BlockSpec((1, gb, SSM_GROUP, 2 * nstate), lambda i, j: (i, j, 0, 0))
    big = pl.BlockSpec((1, gb, rows, 2 * LANES), lambda i, j: (i, j, 0, 0))
    kblk = pl.BlockSpec((1, gb, rows, SSM_GROUP), lambda i, j: (i, j, 0, 0))
    return pl.pallas_call(
        _ssm_prep_kernel,
        grid=(depth, ngroups // gb),
        in_specs=[vec, vec, vec, mat, mat, mat, mat],
        out_specs=[big, big, kblk, kblk,
                   pl.BlockSpec((1, gb, 2, 2 * nstate), lambda i, j: (i, j, 0, 0))],
        out_shape=[
            jax.ShapeDtypeStruct((depth, ngroups, rows, 2 * LANES), F32),
            jax.ShapeDtypeStruct((depth, ngroups, rows, 2 * LANES), F32),
            jax.ShapeDtypeStruct((depth, ngroups, rows, SSM_GROUP), F32),
            jax.ShapeDtypeStruct((depth, ngroups, rows, SSM_GROUP), F32),
            jax.ShapeDtypeStruct((depth, ngroups, 2, 2 * nstate), F32),
        ],
        compiler_params=_cparams("parallel", "parallel"),
        name="ssm_prep",
    )(lr, li, ldt, bt_re, bt_im, cc_re, cc_im)


def _toeplitz_layout(kf, kb):
    t = SSM_CHUNK
    lead = kf.shape[:-2]
    kf = kf.reshape(*lead, t, SSM_GROUP, SSM_GROUP)
    kb = kb.reshape(*lead, t, SSM_GROUP, SSM_GROUP)
    s_idx = jnp.arange(t)[:, None]
    t_idx = jnp.arange(t)[None, :]
    lag = t_idx - s_idx
    mf = jnp.take(kf, jnp.clip(lag, 0, t - 1), axis=-3)
    mb = jnp.take(kb, jnp.clip(-lag, 0, t - 1), axis=-3)
    m = jnp.where((lag >= 0)[:, :, None, None], mf, mb)
    m = jnp.swapaxes(m, -3, -2)
    return m.reshape(*lead, t * SSM_GROUP, t * SSM_GROUP)


def _ssm_kernel(u_ref, w_ref, m_ref, vt_ref, a_ref, d_ref, y_ref, s_scr, h_scr, *, bsz, nchunks):
    ngrp = u_ref.shape[0]
    for gi in range(ngrp):
        ub = u_ref[gi].astype(BF16)
        s_scr[gi] = jnp.dot(ub, w_ref[gi], preferred_element_type=F32)

    lane = lax.broadcasted_iota(jnp.int32, (bsz, LANES), 1)
    fwd = lane < SSM_STATE
    a_re = [jnp.broadcast_to(a_ref[gi, 0:1, :], (bsz, LANES)) for gi in range(ngrp)]
    a_im = [jnp.broadcast_to(a_ref[gi, 1:2, :], (bsz, LANES)) for gi in range(ngrp)]

    def step(j, carry):
        rf = pl.multiple_of(j * bsz, bsz)
        rb = pl.multiple_of((nchunks - 1 - j) * bsz, bsz)
        out = []
        for gi in range(ngrp):
            h_re, h_im = carry[gi]
            h_scr[gi, pl.ds(rf, bsz), 0:SSM_STATE] = h_re[:, 0:SSM_STATE]
            h_scr[gi, pl.ds(rb, bsz), SSM_STATE:LANES] = h_re[:, SSM_STATE:LANES]
            h_scr[gi, pl.ds(rf, bsz), LANES:LANES + SSM_STATE] = h_im[:, 0:SSM_STATE]
            h_scr[gi, pl.ds(rb, bsz), LANES + SSM_STATE:2 * LANES] = h_im[:, SSM_STATE:LANES]
            s_re = jnp.where(fwd, s_scr[gi, pl.ds(rf, bsz), 0:LANES], s_scr[gi, pl.ds(rb, bsz), 0:LANES])
            s_im = jnp.where(fwd, s_scr[gi, pl.ds(rf, bsz), LANES:2 * LANES],
                             s_scr[gi, pl.ds(rb, bsz), LANES:2 * LANES])
            n_re = a_re[gi] * h_re - a_im[gi] * h_im + s_re
            n_im = a_re[gi] * h_im + a_im[gi] * h_re + s_im
            out.append((n_re, n_im))
        return tuple(out)

    zero = jnp.zeros((bsz, LANES), F32)
    lax.fori_loop(0, nchunks, step, tuple((zero, zero) for _ in range(ngrp)))

    for gi in range(ngrp):
        u = u_ref[gi]
        y = jnp.dot(u.astype(BF16), m_ref[gi], preferred_element_type=F32)
        y = y + lax.dot_general(h_scr[gi].astype(BF16), vt_ref[gi], _DN_NT, preferred_element_type=F32)
        y_ref[gi] = y + u * d_ref[gi]


def _ssm_mix(uc, w, m, vt, a16, dtile, bsz):
    ngroups, rows, width = uc.shape
    nchunks = rows // bsz
    gb = min(SSM_GROUPS_PER_STEP, ngroups)
    blk = lambda shape: pl.BlockSpec((gb,) + shape, lambda i: (i,) + (0,) * len(shape))
    return pl.pallas_call(
        functools.partial(_ssm_kernel, bsz=bsz, nchunks=nchunks),
        grid=(ngroups // gb,),
        in_specs=[blk((rows, width)), blk((width, 2 * LANES)), blk((width, width)),
                  blk((width, 2 * LANES)), blk((2, LANES)), blk((1, width))],
        out_specs=blk((rows, width)),
        out_shape=jax.ShapeDtypeStruct(uc.shape, F32),
        scratch_shapes=[pltpu.VMEM((gb, rows, 2 * LANES), F32), pltpu.VMEM((gb, rows, 2 * LANES), F32)],
        compiler_params=_cparams("parallel"),
        name="ssm_mix",
    )(uc, w, m, vt, a16, dtile)


def _to_chunks(u):
    bsz, seq, width = u.shape
    ngroups = width // SSM_GROUP
    nchunks = seq // SSM_CHUNK
    u5 = u.reshape(bsz, nchunks, SSM_CHUNK, ngroups, SSM_GROUP)
    return u5.transpose(3, 1, 0, 2, 4).reshape(ngroups, nchunks * bsz, SSM_CHUNK * SSM_GROUP)


def _from_chunks(y, bsz):
    ngroups, rows, _ = y.shape
    nchunks = rows // bsz
    y5 = y.reshape(ngroups, nchunks, bsz, SSM_CHUNK, SSM_GROUP)
    return y5.transpose(2, 1, 3, 0, 4).reshape(bsz, nchunks * SSM_CHUNK, ngroups * SSM_GROUP)


def _mixout_kernel(oa_ref, ys_ref, x_ref, mod_ref, wg_ref, bg_ref, an_ref, sn_ref, wo_ref, o_ref):
    half = oa_ref.shape[2]
    m = mod_ref[0, 0]
    y = jax.nn.gelu(ys_ref[0])
    gate = jax.nn.sigmoid(jnp.dot(y.astype(BF16), wg_ref[...], preferred_element_type=F32) + bg_ref[...])
    o_ssm = y * gate
    na = (_rms(oa_ref[0]) * an_ref[...]).astype(BF16)
    ns = (_rms(o_ssm) * sn_ref[...]).astype(BF16)
    proj = jnp.dot(na, wo_ref[0:half, :], preferred_element_type=F32)
    proj = proj + jnp.dot(ns, wo_ref[half:2 * half, :], preferred_element_type=F32)
    o_ref[0] = x_ref[0] + m[2:3] * proj


def _mixout(o_attn, y_ssm, x, mod6, layer, w_glu, b_glu, an, sn, w_out):
    bsz, seq, d = x.shape
    half = o_attn.shape[2]
    tm = min(TOKEN_TILE, seq)
    row = lambda b, i: (b, i, 0)
    const = lambda b, i: (0, 0)
    return pl.pallas_call(
        _mixout_kernel,
        grid=(bsz, seq // tm),
        in_specs=[
            pl.BlockSpec((1, tm, half), row),
            pl.BlockSpec((1, tm, half), row),
            pl.BlockSpec((1, tm, d), row),
            pl.BlockSpec((1, 1, N_MOD, d), lambda b, i: (layer, b, 0, 0)),
            pl.BlockSpec((half, half), const),
            pl.BlockSpec((1, half), const),
            pl.BlockSpec((1, half), const),
            pl.BlockSpec((1, half), const),
            pl.BlockSpec((2 * half, d), const),
        ],
        out_specs=pl.BlockSpec((1, tm, d), row),
        out_shape=jax.ShapeDtypeStruct(x.shape, F32),
        compiler_params=_cparams("parallel", "parallel"),
        name="mixout",
    )(o_attn, y_ssm, x, mod6, w_glu, b_glu, an, sn, w_out)


def _ffn_kernel(x_ref, mod_ref, n2_ref, wi_ref, wo_ref, fn_ref, o_ref, *, final):
    hidden = wo_ref.shape[0]
    x = x_ref[0]
    m = mod_ref[0, 0]
    h = _rms(x) * n2_ref[...]
    h = h * (1.0 + m[4:5]) + m[3:4]
    gu = jnp.dot(h.astype(BF16), wi_ref[...], preferred_element_type=F32)
    gt = gu[:, 0:hidden]
    act = (gt * jax.nn.sigmoid(gt)) * gu[:, hidden:2 * hidden]
    y = x + m[5:6] * jnp.dot(act.astype(BF16), wo_ref[...], preferred_element_type=F32)
    if final:
        y = _rms(y) * fn_ref[...]
    o_ref[0] = y


def _ffn(x, mod6, layer, n2, w_in, w_out, final_norm, final):
    bsz, seq, d = x.shape
    hidden = w_out.shape[0]
    tm = min(FFN_TOKEN_TILE, seq)
    row = lambda b, i: (b, i, 0)
    const = lambda b, i: (0, 0)
    return pl.pallas_call(
        functools.partial(_ffn_kernel, final=final),
        grid=(bsz, seq // tm),
        in_specs=[
            pl.BlockSpec((1, tm, d), row),
            pl.BlockSpec((1, 1, N_MOD, d), lambda b, i: (layer, b, 0, 0)),
            pl.BlockSpec((1, d), const),
            pl.BlockSpec((d, 2 * hidden), const),
            pl.BlockSpec((hidden, d), const),
            pl.BlockSpec((1, d), const),
        ],
        out_specs=pl.BlockSpec((1, tm, d), row),
        out_shape=jax.ShapeDtypeStruct(x.shape, F32),
        compiler_params=_cparams("parallel", "parallel"),
        name="ffn",
    )(x, mod6, n2, w_in, w_out, final_norm)


def _pair_split_perm():
    return jnp.concatenate([jnp.arange(0, HEAD_DIM, 2), jnp.arange(1, HEAD_DIM, 2)])


def _expand_w_in(w_in):
    d = w_in.shape[0]
    perm = _pair_split_perm()
    rep = N_Q_HEADS // N_KV_HEADS
    zeros = jnp.zeros((d, HEAD_DIM), w_in.dtype)
    cols = []
    for h in range(N_Q_HEADS):
        wh = w_in[:, h * HEAD_DIM:(h + 1) * HEAD_DIM][:, perm]
        cols += [wh, zeros] if h // rep == 0 else [zeros, wh]
    k0 = N_Q_HEADS * HEAD_DIM
    for g in range(N_KV_HEADS):
        cols.append(w_in[:, k0 + g * HEAD_DIM:k0 + (g + 1) * HEAD_DIM][:, perm])
    cols.append(w_in[:, k0 + N_KV_HEADS * HEAD_DIM:])
    return jnp.concatenate(cols, axis=1).astype(BF16)


def _rope_tables(seq):
    rows = seq // GRID_W
    row_idx = jnp.repeat(jnp.arange(rows, dtype=F32), GRID_W)
    col_idx = jnp.tile(jnp.arange(GRID_W, dtype=F32), rows)
    inv_freq = 1.0 / (ROPE_THETA ** (jnp.arange(0, HALF_HEAD, 2, dtype=F32) / HALF_HEAD))
    ang = jnp.concatenate([row_idx[:, None] * inv_freq, col_idx[:, None] * inv_freq], axis=-1)
    cos, sin = jnp.cos(ang), jnp.sin(ang)
    reps = LANES // HEAD_DIM
    cos_t = jnp.tile(jnp.concatenate([cos, cos], axis=-1), (1, reps))
    sin_t = jnp.tile(jnp.concatenate([-sin, sin], axis=-1), (1, reps))
    return cos_t, sin_t


def kernel(x, c, w_ada, b_ada, norm1, w_in, q_norm, k_norm, ssm_lam_re, ssm_lam_im, ssm_log_dt, ssm_b_re, ssm_b_im, ssm_c_re, ssm_c_im, ssm_d, w_glu, b_glu, attn_out_norm, ssm_out_norm, w_out, norm2, w_ffn_in, w_ffn_out, final_norm):
    bsz, seq, d = x.shape
    depth = w_in.shape[0]
    assert seq % min(TOKEN_TILE, seq) == 0 and seq % GRID_W == 0 and bsz % SUBLANES == 0

    mod6 = _modulation(c, w_ada, b_ada).reshape(depth, bsz, N_MOD, d)
    cos_t, sin_t = _rope_tables(seq)
    perm = _pair_split_perm()
    reps = LANES // HEAD_DIM
    blk = jnp.arange(LANES) // HEAD_DIM
    ones_blk = (blk[:, None] == blk[None, :]).astype(BF16)

    w_s, vt_s, kf, kb, a16 = _ssm_prep(ssm_lam_re, ssm_lam_im, ssm_log_dt, ssm_b_re, ssm_b_im, ssm_c_re, ssm_c_im)
    m_s = _toeplitz_layout(kf, kb).astype(BF16)
    w_s = w_s.astype(BF16)
    vt_s = vt_s.astype(BF16)
    ngroups = ssm_lam_re.shape[2]
    dtile = jnp.tile(ssm_d.reshape(depth, ngroups, 1, SSM_GROUP), (1, 1, 1, SSM_CHUNK))

    row2 = lambda a: a.reshape(1, -1)
    for i in range(depth):
        qg = row2(jnp.tile(q_norm[i][perm], reps))
        kg = row2(jnp.tile(k_norm[i][perm], reps))
        q, k, v, u = _premix(x, mod6, i, row2(norm1[i]), _expand_w_in(w_in[i]), qg, kg, cos_t, sin_t, ones_blk)
        o_attn = _attention(q, k, v)
        y = _ssm_mix(_to_chunks(u), w_s[i], m_s[i], vt_s[i], a16[i], dtile[i], bsz)
        x = _mixout(o_attn, _from_chunks(y, bsz), x, mod6, i, w_glu[i].astype(BF16), row2(b_glu[i]),
                    row2(attn_out_norm[i]), row2(ssm_out_norm[i]), w_out[i].astype(BF16))
        x = _ffn(x, mod6, i, row2(norm2[i]), w_ffn_in[i].astype(BF16), w_ffn_out[i].astype(BF16),
                 row2(final_norm), final=(i == depth - 1))
    return x
```

```python
import functools

import jax
import jax.numpy as jnp
from jax import lax
from jax.experimental import pallas as pl
from jax.experimental.pallas import tpu as pltpu

F32 = jnp.float32
BF16 = jnp.bfloat16

N_Q_HEADS = 8
N_KV_HEADS = 2
HEAD_DIM = 64
HALF_HEAD = HEAD_DIM // 2
GRID_W = 64
ROPE_THETA = 10000.0
SSM_GROUP = 16
SSM_STATE = 64
LAMBDA_RE_MAX = -1e-4
NORM_EPS = 1e-6
N_MOD = 6

LANES = 128
SUBLANES = 8
SSM_CHUNK = 16
TOKEN_TILE = 512
FFN_TOKEN_TILE = 256
ATTN_Q_BLOCK = 1024
Q_TILE = 64
SSM_GROUPS_PER_STEP = 4
PREP_GROUPS_PER_STEP = 8
VMEM_LIMIT = 56 * 1024 * 1024

_DN_NT = (((1,), (1,)), ((), ()))


def _cparams(*sem):
    return pltpu.CompilerParams(dimension_semantics=sem, vmem_limit_bytes=VMEM_LIMIT)


def _rms(x):
    return x * lax.rsqrt(jnp.mean(x * x, axis=-1, keepdims=True) + NORM_EPS)


def _mod_kernel(c_ref, w_ref, b_ref, o_ref):
    c = c_ref[...]
    ca = (c * jax.nn.sigmoid(c)).astype(BF16)
    o_ref[0] = jnp.dot(ca, w_ref[0].astype(BF16), preferred_element_type=F32) + b_ref[0]


def _modulation(c, w_ada, b_ada):
    depth, d, n = w_ada.shape
    bsz = c.shape[0]
    tn = 1536
    return pl.pallas_call(
        _mod_kernel,
        grid=(depth, n // tn),
        in_specs=[
            pl.BlockSpec((bsz, d), lambda i, j: (0, 0)),
            pl.BlockSpec((1, d, tn), lambda i, j: (i, 0, j)),
            pl.BlockSpec((1, 1, tn), lambda i, j: (i, 0, j)),
        ],
        out_specs=pl.BlockSpec((1, bsz, tn), lambda i, j: (i, 0, j)),
        out_shape=jax.ShapeDtypeStruct((depth, bsz, n), F32),
        compiler_params=_cparams("parallel", "parallel"),
        name="adaln_mod",
    )(c, w_ada, b_ada.reshape(depth, 1, n))


def _norm_rope(zt, gain, cos, sin, ones_blk):
    ssq = jnp.dot((zt * zt).astype(BF16), ones_blk, preferred_element_type=F32)
    y = zt * lax.rsqrt(ssq * (1.0 / HEAD_DIM) + NORM_EPS) * gain
    lane = lax.broadcasted_iota(jnp.int32, y.shape, 1)
    partner = jnp.where((lane & HALF_HEAD) == 0,
                        pltpu.roll(y, LANES - HALF_HEAD, 1), pltpu.roll(y, HALF_HEAD, 1))
    return y * cos + partner * sin


def _granule_transpose(xs):
    lane = lax.broadcasted_iota(jnp.int32, xs[0].shape, 1)
    n = len(xs)
    xs = list(xs)
    for bit in (2, 1, 0):
        d = 1 << bit
        shift = d * SSM_GROUP
        hi = ((lane // SSM_GROUP) & d) != 0
        new = list(xs)
        for a in range(n):
            if a & d:
                continue
            b = a | d
            new[a] = jnp.where(hi, pltpu.roll(xs[b], shift, 1), xs[a])
            new[b] = jnp.where(hi, xs[b], pltpu.roll(xs[a], LANES - shift, 1))
        xs = new
    return xs


def _premix_kernel(x_ref, mod_ref, n1_ref, w_ref, qg_ref, kg_ref, cos_ref, sin_ref, ones_ref,
                   q_ref, k_ref, v_ref, uc_ref, u_scr):
    x = x_ref[0]
    m = mod_ref[0, 0]
    h = _rms(x) * n1_ref[...]
    h = h * (1.0 + m[1:2]) + m[0:1]
    z = jnp.dot(h.astype(BF16), w_ref[...], preferred_element_type=F32)
    cos = cos_ref[...]
    sin = sin_ref[...]
    ones_blk = ones_ref[...]
    qg = qg_ref[...] * (HEAD_DIM ** -0.5)
    nq = N_Q_HEADS * LANES
    for hd in range(N_Q_HEADS):
        zt = z[:, hd * LANES:(hd + 1) * LANES]
        q_ref[0, :, hd * LANES:(hd + 1) * LANES] = _norm_rope(zt, qg, cos, sin, ones_blk).astype(BF16)
    k_ref[0] = _norm_rope(z[:, nq:nq + LANES], kg_ref[...], cos, sin, ones_blk).astype(BF16)
    v = z[:, nq + LANES:nq + 2 * LANES]
    v_ref[0, :, 0:LANES] = v.astype(BF16)
    v_ref[0, :, LANES:2 * LANES] = jnp.ones_like(v).astype(BF16)
    ntiles, tm, _ = u_scr.shape
    nk = tm // SSM_CHUNK
    gran = LANES // SSM_GROUP
    u0 = nq + 2 * LANES
    for tile in range(ntiles):
        u_t = u_scr.at[tile]
        u_t[...] = z[:, u0 + tile * LANES:u0 + (tile + 1) * LANES]
        for half in range(SSM_CHUNK // gran):
            xs = [u_t[pl.ds(half * gran + s, nk, stride=SSM_CHUNK), :] for s in range(gran)]
            for g, yg in enumerate(_granule_transpose(xs)):
                uc_ref[tile * gran + g, :, half * LANES:(half + 1) * LANES] = yg


def _premix(x, mod6, layer, n1, w_exp, qg, kg, cos_t, sin_t, ones_blk):
    bsz, seq, d = x.shape
    nc = w_exp.shape[1]
    nq = N_Q_HEADS * LANES
    nu = nc - nq - 2 * LANES
    tm = min(TOKEN_TILE, seq)
    ntile = seq // tm
    nk = tm // SSM_CHUNK
    ngroups = nu // SSM_GROUP
    width = SSM_CHUNK * SSM_GROUP
    row = lambda b, i: (b, i, 0)
    const = lambda b, i: (0, 0)
    return pl.pallas_call(
        _premix_kernel,
        grid=(bsz, ntile),
        in_specs=[
            pl.BlockSpec((1, tm, d), row),
            pl.BlockSpec((1, 1, N_MOD, d), lambda b, i: (layer, b, 0, 0)),
            pl.BlockSpec((1, d), const),
            pl.BlockSpec((d, nc), const),
            pl.BlockSpec((1, LANES), const),
            pl.BlockSpec((1, LANES), const),
            pl.BlockSpec((tm, LANES), lambda b, i: (i, 0)),
            pl.BlockSpec((tm, LANES), lambda b, i: (i, 0)),
            pl.BlockSpec((LANES, LANES), const),
        ],
        out_specs=[
            pl.BlockSpec((1, tm, nq), row),
            pl.BlockSpec((1, tm, LANES), row),
            pl.BlockSpec((1, tm, 2 * LANES), row),
            pl.BlockSpec((ngroups, nk, width), lambda b, i: (0, b * ntile + i, 0)),
        ],
        out_shape=[
            jax.ShapeDtypeStruct((bsz, seq, nq), BF16),
            jax.ShapeDtypeStruct((bsz, seq, LANES), BF16),
            jax.ShapeDtypeStruct((bsz, seq, 2 * LANES), BF16),
            jax.ShapeDtypeStruct((ngroups, bsz * seq // SSM_CHUNK, width), F32),
        ],
        scratch_shapes=[pltpu.VMEM((nu // LANES, tm, LANES), F32)],
        compiler_params=_cparams("parallel", "parallel"),
        name="premix",
    )(x, mod6, n1, w_exp, qg, kg, cos_t, sin_t, ones_blk)


def _attn_kernel(q_ref, k_ref, v_ref, o_ref, s_scr, *, tq):
    nsub = q_ref.shape[1] // tq
    rep = N_Q_HEADS // N_KV_HEADS
    lane = lax.broadcasted_iota(jnp.int32, (tq, LANES), 1)

    def scores(j, slot):
        r0 = pl.multiple_of(j * tq, tq)
        qs = jnp.concatenate(
            [q_ref[0, pl.ds(r0, tq), h * LANES:(h + 1) * LANES] for h in range(N_Q_HEADS)], axis=0)
        s_scr[slot] = lax.dot_general(qs, k_ref[0], _DN_NT, preferred_element_type=F32)

    def finish(j, slot):
        r0 = pl.multiple_of(j * tq, tq)
        s = s_scr[slot]
        e = jnp.exp(s - jnp.max(s, axis=-1, keepdims=True)).astype(BF16)
        pv = jnp.dot(e, v_ref[0], preferred_element_type=F32)
        o = pv[:, 0:LANES] / pv[:, LANES:2 * LANES]
        for t in range(N_Q_HEADS // 2):
            even = o[(2 * t) * tq:(2 * t + 1) * tq]
            odd = o[(2 * t + 1) * tq:(2 * t + 2) * tq]
            if (2 * t) // rep == 0:
                tile = jnp.where(lane < HEAD_DIM, even, pltpu.roll(odd, HEAD_DIM, 1))
            else:
                tile = jnp.where(lane < HEAD_DIM, pltpu.roll(even, HEAD_DIM, 1), odd)
            o_ref[0, pl.ds(r0, tq), t * LANES:(t + 1) * LANES] = tile

    scores(0, 0)

    def pair(i, carry):
        j = 2 * i
        scores(j + 1, 1)
        finish(j, 0)
        scores(j + 2, 0)
        finish(j + 1, 1)
        return carry

    lax.fori_loop(0, nsub // 2 - 1, pair, 0)
    scores(nsub - 1, 1)
    finish(nsub - 2, 0)
    finish(nsub - 1, 1)


def _attention(q, k, v):
    bsz, seq, nq = q.shape
    tb = min(ATTN_Q_BLOCK, seq)
    tq = Q_TILE
    assert tb % (2 * tq) == 0
    width = N_Q_HEADS * HEAD_DIM
    return pl.pallas_call(
        functools.partial(_attn_kernel, tq=tq),
        grid=(bsz, seq // tb),
        in_specs=[
            pl.BlockSpec((1, tb, nq), lambda b, i: (b, i, 0)),
            pl.BlockSpec((1, seq, LANES), lambda b, i: (b, 0, 0)),
            pl.BlockSpec((1, seq, 2 * LANES), lambda b, i: (b, 0, 0)),
        ],
        out_specs=pl.BlockSpec((1, tb, width), lambda b, i: (b, i, 0)),
        out_shape=jax.ShapeDtypeStruct((bsz, seq, width), F32),
        scratch_shapes=[pltpu.VMEM((2, N_Q_HEADS * tq, seq), F32)],
        compiler_params=_cparams("parallel", "parallel"),
        name="attention",
    )(q, k, v)


def _cmul(ar, ai, br, bi):
    return ar * br - ai * bi, ar * bi + ai * br


def _cpow_table(sq_re, sq_im, expo):
    pr = jnp.ones(expo.shape, F32)
    pi = jnp.zeros(expo.shape, F32)
    for bit, (sr, si) in enumerate(zip(sq_re, sq_im)):
        nr, ni = _cmul(pr, pi, sr, si)
        take = (expo & (1 << bit)) != 0
        pr = jnp.where(take, nr, pr)
        pi = jnp.where(take, ni, pi)
    return pr, pi


def _split_bf16(a):
    hi = a.astype(BF16)
    return hi, (a - hi.astype(F32)).astype(BF16)


def _dot_nt_3pass(a, b):
    ah, al = _split_bf16(a)
    bh, bl = _split_bf16(b)
    dg = functools.partial(lax.dot_general, dimension_numbers=_DN_NT, preferred_element_type=F32)
    return dg(ah, bh) + (dg(ah, bl) + dg(al, bh))


def _ssm_prep_kernel(lr_ref, li_ref, ldt_ref, br_ref, bi_ref, cr_ref, ci_ref,
                     w_ref, vt_ref, kf_ref, kb_ref, a_ref):
    ngrp = lr_ref.shape[1]
    t = SSM_CHUNK
    rows = t * SSM_GROUP
    shape = (rows, LANES)
    pos = lax.broadcasted_iota(jnp.int32, shape, 0) // SSM_GROUP
    lane = lax.broadcasted_iota(jnp.int32, shape, 1)
    fwd = lane < SSM_STATE
    lane_c = lax.broadcasted_iota(jnp.int32, (SSM_GROUP, LANES), 1)
    for gi in range(ngrp):
        lr = jnp.minimum(lr_ref[0, gi:gi + 1, :], LAMBDA_RE_MAX)
        li = li_ref[0, gi:gi + 1, :]
        dt = jnp.exp(ldt_ref[0, gi:gi + 1, :])
        mag = jnp.exp(lr * dt)
        a_re = mag * jnp.cos(li * dt)
        a_im = mag * jnp.sin(li * dt)
        den = lr * lr + li * li
        n_re = a_re - 1.0
        k_re = (n_re * lr + a_im * li) / den
        k_im = (a_im * lr - n_re * li) / den
        br = br_ref[0, gi]
        bi = bi_ref[0, gi]
        bb_re = k_re * br - k_im * bi
        bb_im = k_re * bi + k_im * br
        bb_re = jnp.concatenate([bb_re] * t, axis=0)
        bb_im = jnp.concatenate([bb_im] * t, axis=0)
        cr = cr_ref[0, gi]
        ci = ci_ref[0, gi]
        c_re = jnp.concatenate([cr] * t, axis=0)
        c_im = jnp.concatenate([ci] * t, axis=0)
        sq_re, sq_im = [a_re], [a_im]
        for _ in range(4):
            r2, i2 = _cmul(sq_re[-1], sq_im[-1], sq_re[-1], sq_im[-1])
            sq_re.append(r2)
            sq_im.append(i2)
        a_ref[0, gi, 0:1, :] = sq_re[4]
        a_ref[0, gi, 1:2, :] = sq_im[4]
        pr, pi = _cpow_table(sq_re[:4], sq_im[:4], jnp.where(fwd, t - 1 - pos, pos))
        wr, wi = _cmul(pr, pi, bb_re, bb_im)
        w_ref[0, gi, :, 0:LANES] = wr
        w_ref[0, gi, :, LANES:2 * LANES] = wi
        pr, pi = _cpow_table(sq_re, sq_im, jnp.where(fwd, pos + 1, t - pos))
        vr, vi = _cmul(pr, pi, c_re, c_im)
        vt_ref[0, gi, :, 0:LANES] = vr
        vt_ref[0, gi, :, LANES:2 * LANES] = -vi
        pr, pi = _cpow_table(sq_re[:4], sq_im[:4], pos)
        gr, gim = _cmul(pr, pi, bb_re, bb_im)
        zero = jnp.zeros_like(cr)
        cfr = jnp.where(lane_c < SSM_STATE, cr, zero)
        cfi = jnp.where(lane_c < SSM_STATE, ci, zero)
        cbr = jnp.where(lane_c < SSM_STATE, zero, cr)
        cbi = jnp.where(lane_c < SSM_STATE, zero, ci)
        kf = _dot_nt_3pass(gr, cfr) - _dot_nt_3pass(gim, cfi)
        kb = _dot_nt_3pass(gr, cbr) - _dot_nt_3pass(gim, cbi)
        row = lax.broadcasted_iota(jnp.int32, kf.shape, 0)
        kf_ref[0, gi] = kf + jnp.where(row < SSM_GROUP, kb, jnp.zeros_like(kb))
        kb_ref[0, gi] = kb


def _ssm_prep(lam_re, lam_im, log_dt, b_re, b_im, c_re, c_im):
    depth, _, ngroups, nstate = lam_re.shape
    cat = lambda a: jnp.concatenate([a[:, 0], a[:, 1]], axis=-1)
    lr = cat(lam_re)
    li = cat(lam_im)
    ldt = cat(jnp.broadcast_to(log_dt[..., None], lam_re.shape))
    bt_re = cat(jnp.swapaxes(b_re, -1, -2))
    bt_im = cat(jnp.swapaxes(b_im, -1, -2))
    cc_re = cat(c_re)
    cc_im = cat(c_im)
    gb = min(PREP_GROUPS_PER_STEP, ngroups)
    rows = SSM_CHUNK * SSM_GROUP
    vec = pl.BlockSpec((1, gb, 2 * nstate), lambda i, j: (i, j, 0))
    mat = pl.BlockSpec((1, gb, SSM_GROUP, 2 * nstate), lambda i, j: (i, j, 0, 0))
    big = pl.BlockSpec((1, gb, rows, 2 * LANES), lambda i, j: (i, j, 0, 0))
    kblk = pl.BlockSpec((1, gb, rows, SSM_GROUP), lambda i, j: (i, j, 0, 0))
    return pl.pallas_call(
        _ssm_prep_kernel,
        grid=(depth, ngroups // gb),
        in_specs=[vec, vec, vec, mat, mat, mat, mat],
        out_specs=[big, big, kblk, kblk,
                   pl.BlockSpec((1, gb, 2, 2 * nstate), lambda i, j: (i, j, 0, 0))],
        out_shape=[
            jax.ShapeDtypeStruct((depth, ngroups, rows, 2 * LANES), F32),
            jax.ShapeDtypeStruct((depth, ngroups, rows, 2 * LANES), F32),
            jax.ShapeDtypeStruct((depth, ngroups, rows, SSM_GROUP), F32),
            jax.ShapeDtypeStruct((depth, ngroups, rows, SSM_GROUP), F32),
            jax.ShapeDtypeStruct((depth, ngroups, 2, 2 * nstate), F32),
        ],
        compiler_params=_cparams("parallel", "parallel"),
        name="ssm_prep",
    )(lr, li, ldt, bt_re, bt_im, cc_re, cc_im)


def _toeplitz_layout(kf, kb):
    t = SSM_CHUNK
    lead = kf.shape[:-2]
    kf = kf.reshape(*lead, t, SSM_GROUP, SSM_GROUP)
    kb = kb.reshape(*lead, t, SSM_GROUP, SSM_GROUP)
    s_idx = jnp.arange(t)[:, None]
    t_idx = jnp.arange(t)[None, :]
    lag = t_idx - s_idx
    mf = jnp.take(kf, jnp.clip(lag, 0, t - 1), axis=-3)
    mb = jnp.take(kb, jnp.clip(-lag, 0, t - 1), axis=-3)
    m = jnp.where((lag >= 0)[:, :, None, None], mf, mb)
    m = jnp.swapaxes(m, -3, -2)
    return m.reshape(*lead, t * SSM_GROUP, t * SSM_GROUP)


def _ssm_kernel(u_ref, w_ref, m_ref, vt_ref, a_ref, d_ref, y_ref, s_scr, p_scr, *, bsz, nchunks):
    ngrp = u_ref.shape[0]
    rows = u_ref.shape[1]
    nv = nchunks // SUBLANES
    ntile = rows // SUBLANES
    sub = lax.broadcasted_iota(jnp.int32, (rows, LANES), 0) & (SUBLANES - 1)
    fwd = lax.broadcasted_iota(jnp.int32, (rows, LANES), 1) < SSM_STATE
    depth_in_tile = jnp.where(fwd, sub, SUBLANES - 1 - sub)
    fwd_t = lax.broadcasted_iota(jnp.int32, (ntile, LANES), 1) < SSM_STATE
    sub8 = lax.broadcasted_iota(jnp.int32, (SUBLANES, LANES), 0)
    fwd8 = lax.broadcasted_iota(jnp.int32, (SUBLANES, LANES), 1) < SSM_STATE

    def shifted(x, d):
        ok = depth_in_tile >= d
        moved = jnp.where(fwd, pltpu.roll(x, d, 0), pltpu.roll(x, rows - d, 0))
        return jnp.where(ok, moved, jnp.zeros_like(x))

    step8, pw = [], []
    for gi in range(ngrp):
        sq_re, sq_im = [a_ref[gi, 0:1, :]], [a_ref[gi, 1:2, :]]
        for _ in range(3):
            r2, i2 = _cmul(sq_re[-1], sq_im[-1], sq_re[-1], sq_im[-1])
            sq_re.append(r2)
            sq_im.append(i2)
        step8.append((sq_re[3], sq_im[3]))
        pw.append(_cpow_table(sq_re[:3], sq_im[:3], jnp.where(fwd8, sub8, SUBLANES - 1 - sub8)))

        s = jnp.dot(u_ref[gi].astype(BF16), w_ref[gi], preferred_element_type=F32)
        x_re, x_im = s[:, 0:LANES], s[:, LANES:2 * LANES]
        for bit in range(3):
            mr, mi = _cmul(sq_re[bit], sq_im[bit], shifted(x_re, 1 << bit), shifted(x_im, 1 << bit))
            x_re, x_im = x_re + mr, x_im + mi
        for part, x in enumerate((x_re, x_im)):
            s_p = s_scr.at[gi, part]
            p_p = p_scr.at[gi, part]
            s_p[...] = x
            edge = jnp.where(fwd_t, s_p[pl.ds(SUBLANES - 1, ntile, stride=SUBLANES), :],
                             s_p[pl.ds(0, ntile, stride=SUBLANES), :])
            for b in range(bsz):
                p_p[pl.ds(b, nv, stride=bsz), :] = edge[b * nv:(b + 1) * nv]
            s_p[...] = shifted(x, 1)

    def tile_step(i, carry):
        out = []
        for gi in range(ngrp):
            c = carry[gi]
            for b in range(bsz):
                rf = pl.multiple_of((b * nv + i) * SUBLANES, SUBLANES)
                rb = pl.multiple_of((b * nv + nv - 1 - i) * SUBLANES, SUBLANES)
                cb = [jnp.broadcast_to(cp[b:b + 1, :], (SUBLANES, LANES)) for cp in c]
                h = _cmul(pw[gi][0], pw[gi][1], cb[0], cb[1])
                for part in range(2):
                    e = jnp.where(fwd8, s_scr[gi, part, pl.ds(rf, SUBLANES), :],
                                  s_scr[gi, part, pl.ds(rb, SUBLANES), :])
                    hp = h[part] + e
                    s_scr[gi, part, pl.ds(rf, SUBLANES), 0:SSM_STATE] = hp[:, 0:SSM_STATE]
                    s_scr[gi, part, pl.ds(rb, SUBLANES), SSM_STATE:LANES] = hp[:, SSM_STATE:LANES]
            pf = pl.multiple_of(i * bsz, bsz)
            pb = pl.multiple_of((nv - 1 - i) * bsz, bsz)
            n = _cmul(step8[gi][0], step8[gi][1], c[0], c[1])
            out.append(tuple(
                n[part] + jnp.where(fwd8, p_scr[gi, part, pl.ds(pf, bsz), :], p_scr[gi, part, pl.ds(pb, bsz), :])
                for part in range(2)))
        return tuple(out)

    zero = jnp.zeros((bsz, LANES), F32)
    lax.fori_loop(0, nv, tile_step, tuple((zero, zero) for _ in range(ngrp)))

    for gi in range(ngrp):
        u = u_ref[gi]
        y = jnp.dot(u.astype(BF16), m_ref[gi], preferred_element_type=F32)
        h = jnp.concatenate([s_scr[gi, 0], s_scr[gi, 1]], axis=1).astype(BF16)
        y = y + lax.dot_general(h, vt_ref[gi], _DN_NT, preferred_element_type=F32)
        y_ref[gi] = y + u * d_ref[gi]


def _ssm_mix(uc, w, m, vt, a16, dtile, bsz):
    ngroups, rows, width = uc.shape
    nchunks = rows // bsz
    assert bsz == SUBLANES and nchunks % SUBLANES == 0
    gb = min(SSM_GROUPS_PER_STEP, ngroups)
    blk = lambda shape: pl.BlockSpec((gb,) + shape, lambda i: (i,) + (0,) * len(shape))
    return pl.pallas_call(
        functools.partial(_ssm_kernel, bsz=bsz, nchunks=nchunks),
        grid=(ngroups // gb,),
        in_specs=[blk((rows, width)), blk((width, 2 * LANES)), blk((width, width)),
                  blk((width, 2 * LANES)), blk((2, LANES)), blk((1, width))],
        out_specs=blk((rows, width)),
        out_shape=jax.ShapeDtypeStruct(uc.shape, F32),
        scratch_shapes=[pltpu.VMEM((gb, 2, rows, LANES), F32),
                        pltpu.VMEM((gb, 2, rows // SUBLANES, LANES), F32)],
        compiler_params=_cparams("parallel"),
        name="ssm_mix",
    )(uc, w, m, vt, a16, dtile)


def _mixout_kernel(oa_ref, yc_ref, x_ref, mod_ref, wg_ref, bg_ref, an_ref, sn_ref, wo_ref, o_ref, y_scr):
    half = oa_ref.shape[2]
    m = mod_ref[0, 0]
    nk = yc_ref.shape[1]
    gran = LANES // SSM_GROUP
    for tile in range(half // LANES):
        y_t = y_scr.at[tile]
        for half_pos in range(SSM_CHUNK // gran):
            xs = [yc_ref[tile * gran + g, :, half_pos * LANES:(half_pos + 1) * LANES] for g in range(gran)]
            for s, ys in enumerate(_granule_transpose(xs)):
                y_t[pl.ds(half_pos * gran + s, nk, stride=SSM_CHUNK), :] = ys
    y = jax.nn.gelu(jnp.concatenate([y_scr[tile] for tile in range(half // LANES)], axis=1))
    gate = jax.nn.sigmoid(jnp.dot(y.astype(BF16), wg_ref[...], preferred_element_type=F32) + bg_ref[...])
    o_ssm = y * gate
    na = (_rms(oa_ref[0]) * an_ref[...]).astype(BF16)
    ns = (_rms(o_ssm) * sn_ref[...]).astype(BF16)
    proj = jnp.dot(na, wo_ref[0:half, :], preferred_element_type=F32)
    proj = proj + jnp.dot(ns, wo_ref[half:2 * half, :], preferred_element_type=F32)
    o_ref[0] = x_ref[0] + m[2:3] * proj


def _mixout(o_attn, y_chunks, x, mod6, layer, w_glu, b_glu, an, sn, w_out):
    bsz, seq, d = x.shape
    half = o_attn.shape[2]
    tm = min(TOKEN_TILE, seq)
    ntile = seq // tm
    ngroups, _, width = y_chunks.shape
    row = lambda b, i: (b, i, 0)
    const = lambda b, i: (0, 0)
    return pl.pallas_call(
        _mixout_kernel,
        grid=(bsz, ntile),
        in_specs=[
            pl.BlockSpec((1, tm, half), row),
            pl.BlockSpec((ngroups, tm // SSM_CHUNK, width), lambda b, i: (0, b * ntile + i, 0)),
            pl.BlockSpec((1, tm, d), row),
            pl.BlockSpec((1, 1, N_MOD, d), lambda b, i: (layer, b, 0, 0)),
            pl.BlockSpec((half, half), const),
            pl.BlockSpec((1, half), const),
            pl.BlockSpec((1, half), const),
            pl.BlockSpec((1, half), const),
            pl.BlockSpec((2 * half, d), const),
        ],
        out_specs=pl.BlockSpec((1, tm, d), row),
        out_shape=jax.ShapeDtypeStruct(x.shape, F32),
        scratch_shapes=[pltpu.VMEM((half // LANES, tm, LANES), F32)],
        compiler_params=_cparams("parallel", "parallel"),
        name="mixout",
    )(o_attn, y_chunks, x, mod6, w_glu, b_glu, an, sn, w_out)


def _ffn_kernel(x_ref, mod_ref, n2_ref, wi_ref, wo_ref, fn_ref, o_ref, *, final):
    hidden = wo_ref.shape[0]
    x = x_ref[0]
    m = mod_ref[0, 0]
    h = _rms(x) * n2_ref[...]
    h = h * (1.0 + m[4:5]) + m[3:4]
    gu = jnp.dot(h.astype(BF16), wi_ref[...], preferred_element_type=F32)
    gt = gu[:, 0:hidden]
    act = (gt * jax.nn.sigmoid(gt)) * gu[:, hidden:2 * hidden]
    y = x + m[5:6] * jnp.dot(act.astype(BF16), wo_ref[...], preferred_element_type=F32)
    if final:
        y = _rms(y) * fn_ref[...]
    o_ref[0] = y


def _ffn(x, mod6, layer, n2, w_in, w_out, final_norm, final):
    bsz, seq, d = x.shape
    hidden = w_out.shape[0]
    tm = min(FFN_TOKEN_TILE, seq)
    row = lambda b, i: (b, i, 0)
    const = lambda b, i: (0, 0)
    return pl.pallas_call(
        functools.partial(_ffn_kernel, final=final),
        grid=(bsz, seq // tm),
        in_specs=[
            pl.BlockSpec((1, tm, d), row),
            pl.BlockSpec((1, 1, N_MOD, d), lambda b, i: (layer, b, 0, 0)),
            pl.BlockSpec((1, d), const),
            pl.BlockSpec((d, 2 * hidden), const),
            pl.BlockSpec((hidden, d), const),
            pl.BlockSpec((1, d), const),
        ],
        out_specs=pl.BlockSpec((1, tm, d), row),
        out_shape=jax.ShapeDtypeStruct(x.shape, F32),
        compiler_params=_cparams("parallel", "parallel"),
        name="ffn",
    )(x, mod6, n2, w_in, w_out, final_norm)


def _pair_split_perm():
    return jnp.concatenate([jnp.arange(0, HEAD_DIM, 2), jnp.arange(1, HEAD_DIM, 2)])


def _expand_w_in(w_in):
    d = w_in.shape[0]
    perm = _pair_split_perm()
    rep = N_Q_HEADS // N_KV_HEADS
    zeros = jnp.zeros((d, HEAD_DIM), w_in.dtype)
    cols = []
    for h in range(N_Q_HEADS):
        wh = w_in[:, h * HEAD_DIM:(h + 1) * HEAD_DIM][:, perm]
        cols += [wh, zeros] if h // rep == 0 else [zeros, wh]
    k0 = N_Q_HEADS * HEAD_DIM
    for g in range(N_KV_HEADS):
        cols.append(w_in[:, k0 + g * HEAD_DIM:k0 + (g + 1) * HEAD_DIM][:, perm])
    cols.append(w_in[:, k0 + N_KV_HEADS * HEAD_DIM:])
    return jnp.concatenate(cols, axis=1).astype(BF16)


def _rope_tables(seq):
    rows = seq // GRID_W
    row_idx = jnp.repeat(jnp.arange(rows, dtype=F32), GRID_W)
    col_idx = jnp.tile(jnp.arange(GRID_W, dtype=F32), rows)
    inv_freq = 1.0 / (ROPE_THETA ** (jnp.arange(0, HALF_HEAD, 2, dtype=F32) / HALF_HEAD))
    ang = jnp.concatenate([row_idx[:, None] * inv_freq, col_idx[:, None] * inv_freq], axis=-1)
    cos, sin = jnp.cos(ang), jnp.sin(ang)
    reps = LANES // HEAD_DIM
    cos_t = jnp.tile(jnp.concatenate([cos, cos], axis=-1), (1, reps))
    sin_t = jnp.tile(jnp.concatenate([-sin, sin], axis=-1), (1, reps))
    return cos_t, sin_t


def kernel(x, c, w_ada, b_ada, norm1, w_in, q_norm, k_norm, ssm_lam_re, ssm_lam_im, ssm_log_dt, ssm_b_re, ssm_b_im, ssm_c_re, ssm_c_im, ssm_d, w_glu, b_glu, attn_out_norm, ssm_out_norm, w_out, norm2, w_ffn_in, w_ffn_out, final_norm):
    bsz, seq, d = x.shape
    depth = w_in.shape[0]
    assert seq % min(TOKEN_TILE, seq) == 0 and seq % GRID_W == 0 and bsz % SUBLANES == 0

    mod6 = _modulation(c, w_ada, b_ada).reshape(depth, bsz, N_MOD, d)
    cos_t, sin_t = _rope_tables(seq)
    perm = _pair_split_perm()
    reps = LANES // HEAD_DIM
    blk = jnp.arange(LANES) // HEAD_DIM
    ones_blk = (blk[:, None] == blk[None, :]).astype(BF16)

    w_s, vt_s, kf, kb, a16 = _ssm_prep(ssm_lam_re, ssm_lam_im, ssm_log_dt, ssm_b_re, ssm_b_im, ssm_c_re, ssm_c_im)
    m_s = _toeplitz_layout(kf, kb).astype(BF16)
    w_s = w_s.astype(BF16)
    vt_s = vt_s.astype(BF16)
    ngroups = ssm_lam_re.shape[2]
    dtile = jnp.tile(ssm_d.reshape(depth, ngroups, 1, SSM_GROUP), (1, 1, 1, SSM_CHUNK))

    row2 = lambda a: a.reshape(1, -1)
    for i in range(depth):
        qg = row2(jnp.tile(q_norm[i][perm], reps))
        kg = row2(jnp.tile(k_norm[i][perm], reps))
        q, k, v, u = _premix(x, mod6, i, row2(norm1[i]), _expand_w_in(w_in[i]), qg, kg, cos_t, sin_t, ones_blk)
        o_attn = _attention(q, k, v)
        y = _ssm_mix(u, w_s[i], m_s[i], vt_s[i], a16[i], dtile[i], bsz)
        x = _mixout(o_attn, y, x, mod6, i, w_glu[i].astype(BF16), row2(b_glu[i]),
                    row2(attn_out_norm[i]), row2(ssm_out_norm[i]), w_out[i].astype(BF16))
        x = _ffn(x, mod6, i, row2(norm2[i]), w_ffn_in[i].astype(BF16), w_ffn_out[i].astype(BF16),
                 row2(final_norm), final=(i == depth - 1))
    return x
```

```python
import functools

import numpy as np
import jax
import jax.numpy as jnp
from jax import lax
from jax.experimental import pallas as pl
from jax.experimental.pallas import tpu as pltpu

F32 = jnp.float32
BF16 = jnp.bfloat16

N_Q_HEADS = 8
N_KV_HEADS = 2
HEAD_DIM = 64
HALF_HEAD = HEAD_DIM // 2
GRID_W = 64
ROPE_THETA = 10000.0
SSM_GROUP = 16
SSM_STATE = 64
LAMBDA_RE_MAX = -1e-4
NORM_EPS = 1e-6
N_MOD = 6

LANES = 128
SUBLANES = 8
SSM_CHUNK = 32
CHUNK_BITS = SSM_CHUNK.bit_length() - 1
CHUNK_WIDTH = SSM_CHUNK * SSM_GROUP
CHUNK_PITCH = SSM_CHUNK + SUBLANES
TOKEN_TILE = 512
FFN_TOKEN_TILE = 256
ATTN_Q_BLOCK = 1024
Q_TILE = 64
SSM_GROUPS_PER_STEP = 4
PREP_GROUPS_PER_STEP = 8
VMEM_LIMIT = 56 * 1024 * 1024

_DN_NT = (((1,), (1,)), ((), ()))


def _cparams(*sem):
    return pltpu.CompilerParams(dimension_semantics=sem, vmem_limit_bytes=VMEM_LIMIT)


def _rms(x):
    return x * lax.rsqrt(jnp.mean(x * x, axis=-1, keepdims=True) + NORM_EPS)


def _layer_spec(shape, layer):
    return pl.BlockSpec((1,) + shape, lambda b, i: (layer,) + (0,) * len(shape))


def _mod_kernel(c_ref, w_ref, b_ref, o_ref):
    c = c_ref[...]
    ca = (c * jax.nn.sigmoid(c)).astype(BF16)
    o_ref[0] = jnp.dot(ca, w_ref[0].astype(BF16), preferred_element_type=F32) + b_ref[0]


def _modulation(c, w_ada, b_ada):
    depth, d, n = w_ada.shape
    bsz = c.shape[0]
    tn = 1536
    return pl.pallas_call(
        _mod_kernel,
        grid=(depth, n // tn),
        in_specs=[
            pl.BlockSpec((bsz, d), lambda i, j: (0, 0)),
            pl.BlockSpec((1, d, tn), lambda i, j: (i, 0, j)),
            pl.BlockSpec((1, 1, tn), lambda i, j: (i, 0, j)),
        ],
        out_specs=pl.BlockSpec((1, bsz, tn), lambda i, j: (i, 0, j)),
        out_shape=jax.ShapeDtypeStruct((depth, bsz, n), F32),
        compiler_params=_cparams("parallel", "parallel"),
        name="adaln_mod",
    )(c, w_ada, b_ada.reshape(depth, 1, n))


def _granule_transpose(xs):
    lane = lax.broadcasted_iota(jnp.int32, xs[0].shape, 1)
    n = len(xs)
    xs = list(xs)
    for bit in (2, 1, 0):
        d = 1 << bit
        shift = d * SSM_GROUP
        hi = ((lane // SSM_GROUP) & d) != 0
        new = list(xs)
        for a in range(n):
            if a & d:
                continue
            b = a | d
            new[a] = jnp.where(hi, pltpu.roll(xs[b], shift, 1), xs[a])
            new[b] = jnp.where(hi, xs[b], pltpu.roll(xs[a], LANES - shift, 1))
        xs = new
    return xs


def _norm_rope(zt, gain, cos, sin, ones_blk):
    ssq = jnp.dot((zt * zt).astype(BF16), ones_blk, preferred_element_type=F32)
    y = zt * lax.rsqrt(ssq * (1.0 / HEAD_DIM) + NORM_EPS) * gain
    lane = lax.broadcasted_iota(jnp.int32, y.shape, 1)
    partner = jnp.where((lane & HALF_HEAD) == 0,
                        pltpu.roll(y, LANES - HALF_HEAD, 1), pltpu.roll(y, HALF_HEAD, 1))
    return y * cos + partner * sin


def _premix_kernel(x_ref, mod_ref, n1_ref, w_ref, qg_ref, kg_ref, cos_ref, sin_ref, ones_ref,
                   q_ref, k_ref, v_ref, uc_ref, u_scr):
    x = x_ref[0]
    m = mod_ref[0, 0]
    h = _rms(x) * n1_ref[0]
    h = h * (1.0 + m[1:2]) + m[0:1]
    z = jnp.dot(h.astype(BF16), w_ref[0], preferred_element_type=F32)
    cos = cos_ref[...]
    sin = sin_ref[...]
    ones_blk = ones_ref[...]
    qg = qg_ref[0] * (HEAD_DIM ** -0.5)
    nq = q_ref.shape[2]
    for t in range(nq // LANES):
        zt = z[:, t * LANES:(t + 1) * LANES]
        q_ref[0, :, t * LANES:(t + 1) * LANES] = _norm_rope(zt, qg, cos, sin, ones_blk).astype(BF16)
    k_ref[0] = _norm_rope(z[:, nq:nq + LANES], kg_ref[0], cos, sin, ones_blk).astype(BF16)
    v = z[:, nq + LANES:nq + 2 * LANES]
    v_ref[0, :, 0:LANES] = v.astype(BF16)
    v_ref[0, :, LANES:2 * LANES] = jnp.ones_like(v).astype(BF16)
    ntiles = u_scr.shape[0]
    nk = uc_ref.shape[1]
    gran = LANES // SSM_GROUP
    u0 = nq + 2 * LANES
    for tile in range(ntiles):
        u_t = u_scr.at[tile]
        for kc in range(nk):
            u_t[kc * CHUNK_PITCH:kc * CHUNK_PITCH + SSM_CHUNK, :] = (
                z[kc * SSM_CHUNK:(kc + 1) * SSM_CHUNK, u0 + tile * LANES:u0 + (tile + 1) * LANES])
        for octet in range(SSM_CHUNK // gran):
            xs = [u_t[pl.ds(octet * gran + s, nk, stride=CHUNK_PITCH), :] for s in range(gran)]
            for g, yg in enumerate(_granule_transpose(xs)):
                uc_ref[tile * gran + g, :, octet * LANES:(octet + 1) * LANES] = yg


def _premix(x, mod6, layer, n1, w_arr, qg, kg, cos_t, sin_t, ones_blk):
    bsz, seq, d = x.shape
    nc = w_arr.shape[2]
    nq = N_Q_HEADS * HEAD_DIM
    nu = nc - nq - 2 * LANES
    tm = min(TOKEN_TILE, seq)
    ntile = seq // tm
    nk = tm // SSM_CHUNK
    ngroups = nu // SSM_GROUP
    row = lambda b, i: (b, i, 0)
    const = lambda b, i: (0, 0)
    return pl.pallas_call(
        _premix_kernel,
        grid=(bsz, ntile),
        in_specs=[
            pl.BlockSpec((1, tm, d), row),
            pl.BlockSpec((1, 1, N_MOD, d), lambda b, i: (layer, b, 0, 0)),
            _layer_spec((1, d), layer),
            _layer_spec((d, nc), layer),
            _layer_spec((1, LANES), layer),
            _layer_spec((1, LANES), layer),
            pl.BlockSpec((tm, LANES), lambda b, i: (i, 0)),
            pl.BlockSpec((tm, LANES), lambda b, i: (i, 0)),
            pl.BlockSpec((LANES, LANES), const),
        ],
        out_specs=[
            pl.BlockSpec((1, tm, nq), row),
            pl.BlockSpec((1, tm, LANES), row),
            pl.BlockSpec((1, tm, 2 * LANES), row),
            pl.BlockSpec((ngroups, nk, CHUNK_WIDTH), lambda b, i: (0, b * ntile + i, 0)),
        ],
        out_shape=[
            jax.ShapeDtypeStruct((bsz, seq, nq), BF16),
            jax.ShapeDtypeStruct((bsz, seq, LANES), BF16),
            jax.ShapeDtypeStruct((bsz, seq, 2 * LANES), BF16),
            jax.ShapeDtypeStruct((ngroups, bsz * seq // SSM_CHUNK, CHUNK_WIDTH), F32),
        ],
        scratch_shapes=[pltpu.VMEM((nu // LANES, nk * CHUNK_PITCH, LANES), F32)],
        compiler_params=_cparams("parallel", "parallel"),
        name="premix",
    )(x, mod6, n1, w_arr, qg, kg, cos_t, sin_t, ones_blk)


def _attn_kernel(q_ref, k_ref, v_ref, o_ref, s_scr, *, tq):
    nsub = q_ref.shape[1] // tq
    rep = N_Q_HEADS // N_KV_HEADS
    lane = lax.broadcasted_iota(jnp.int32, (tq, LANES), 1)
    keep_lo = (lane < HEAD_DIM).astype(BF16)
    keep_hi = (lane >= HEAD_DIM).astype(BF16)

    def scores(j, slot):
        r0 = pl.multiple_of(j * tq, tq)
        qs = jnp.concatenate(
            [q_ref[0, pl.ds(r0, tq), (h % rep) * LANES:(h % rep + 1) * LANES] * (keep_lo if h < rep else keep_hi)
             for h in range(N_Q_HEADS)], axis=0)
        s_scr[slot] = lax.dot_general(qs, k_ref[0], _DN_NT, preferred_element_type=F32)

    def finish(j, slot):
        r0 = pl.multiple_of(j * tq, tq)
        s = s_scr[slot]
        e = jnp.exp(s - jnp.max(s, axis=-1, keepdims=True)).astype(BF16)
        pv = jnp.dot(e, v_ref[0], preferred_element_type=F32)
        o = pv[:, 0:LANES] / pv[:, LANES:2 * LANES]
        for t in range(rep):
            tile = jnp.where(lane < HEAD_DIM, o[t * tq:(t + 1) * tq], o[(t + rep) * tq:(t + rep + 1) * tq])
            o_ref[0, pl.ds(r0, tq), t * LANES:(t + 1) * LANES] = tile

    scores(0, 0)

    def pair(i, carry):
        j = 2 * i
        scores(j + 1, 1)
        finish(j, 0)
        scores(j + 2, 0)
        finish(j + 1, 1)
        return carry

    lax.fori_loop(0, nsub // 2 - 1, pair, 0)
    scores(nsub - 1, 1)
    finish(nsub - 2, 0)
    finish(nsub - 1, 1)


def _attention(q, k, v):
    bsz, seq, nq = q.shape
    tb = min(ATTN_Q_BLOCK, seq)
    tq = Q_TILE
    assert tb % (2 * tq) == 0
    return pl.pallas_call(
        functools.partial(_attn_kernel, tq=tq),
        grid=(bsz, seq // tb),
        in_specs=[
            pl.BlockSpec((1, tb, nq), lambda b, i: (b, i, 0)),
            pl.BlockSpec((1, seq, LANES), lambda b, i: (b, 0, 0)),
            pl.BlockSpec((1, seq, 2 * LANES), lambda b, i: (b, 0, 0)),
        ],
        out_specs=pl.BlockSpec((1, tb, nq), lambda b, i: (b, i, 0)),
        out_shape=jax.ShapeDtypeStruct((bsz, seq, nq), F32),
        scratch_shapes=[pltpu.VMEM((2, N_Q_HEADS * tq, seq), F32)],
        compiler_params=_cparams("parallel", "parallel"),
        name="attention",
    )(q, k, v)


def _cmul(ar, ai, br, bi):
    return ar * br - ai * bi, ar * bi + ai * br


def _cpow_table(sq_re, sq_im, expo):
    pr = jnp.ones(expo.shape, F32)
    pi = jnp.zeros(expo.shape, F32)
    for bit, (sr, si) in enumerate(zip(sq_re, sq_im)):
        nr, ni = _cmul(pr, pi, sr, si)
        take = (expo & (1 << bit)) != 0
        pr = jnp.where(take, nr, pr)
        pi = jnp.where(take, ni, pi)
    return pr, pi


def _split_bf16(a):
    hi = a.astype(BF16)
    return hi, (a - hi.astype(F32)).astype(BF16)


def _dot_nt_3pass(a, b):
    ah, al = _split_bf16(a)
    bh, bl = _split_bf16(b)
    dg = functools.partial(lax.dot_general, dimension_numbers=_DN_NT, preferred_element_type=F32)
    return dg(ah, bh) + (dg(ah, bl) + dg(al, bh))


def _ssm_prep_kernel(lr_ref, li_ref, ldt_ref, br_ref, bi_ref, cr_ref, ci_ref,
                     w_ref, vt_ref, m_ref, a_ref):
    ngrp = lr_ref.shape[1]
    t = SSM_CHUNK
    shape = (CHUNK_WIDTH, LANES)
    pos = lax.broadcasted_iota(jnp.int32, shape, 0) // SSM_GROUP
    fwd = lax.broadcasted_iota(jnp.int32, shape, 1) < SSM_STATE
    for gi in range(ngrp):
        lr = jnp.minimum(lr_ref[0, gi:gi + 1, :], LAMBDA_RE_MAX)
        li = li_ref[0, gi:gi + 1, :]
        dt = jnp.exp(ldt_ref[0, gi:gi + 1, :])
        mag = jnp.exp(lr * dt)
        a_re = mag * jnp.cos(li * dt)
        a_im = mag * jnp.sin(li * dt)
        den = lr * lr + li * li
        n_re = a_re - 1.0
        k_re = (n_re * lr + a_im * li) / den
        k_im = (a_im * lr - n_re * li) / den
        br = br_ref[0, gi]
        bi = bi_ref[0, gi]
        bt_re = k_re * br - k_im * bi
        bt_im = k_re * bi + k_im * br
        bb_re = jnp.concatenate([bt_re] * t, axis=0)
        bb_im = jnp.concatenate([bt_im] * t, axis=0)
        c_re = jnp.concatenate([cr_ref[0, gi]] * t, axis=0)
        c_im = jnp.concatenate([ci_ref[0, gi]] * t, axis=0)
        sq_re, sq_im = [a_re], [a_im]
        for _ in range(CHUNK_BITS):
            r2, i2 = _cmul(sq_re[-1], sq_im[-1], sq_re[-1], sq_im[-1])
            sq_re.append(r2)
            sq_im.append(i2)
        a_ref[0, gi, 0:1, :] = sq_re[CHUNK_BITS]
        a_ref[0, gi, 1:2, :] = sq_im[CHUNK_BITS]
        pr, pi = _cpow_table(sq_re[:CHUNK_BITS], sq_im[:CHUNK_BITS], jnp.where(fwd, t - 1 - pos, pos))
        wr, wi = _cmul(pr, pi, bb_re, bb_im)
        w_ref[0, gi, :, 0:LANES] = wr.astype(BF16)
        w_ref[0, gi, :, LANES:2 * LANES] = wi.astype(BF16)
        pr, pi = _cpow_table(sq_re, sq_im, jnp.where(fwd, pos + 1, t - pos))
        vr, vi = _cmul(pr, pi, c_re, c_im)
        vt_ref[0, gi, :, 0:LANES] = vr.astype(BF16)
        vt_ref[0, gi, :, LANES:2 * LANES] = (-vi).astype(BF16)
        zero = jnp.zeros(shape, F32)
        left_re = jnp.where(fwd, zero, vr)
        left_im = jnp.where(fwd, zero, vi)
        pr, pi = _cpow_table(sq_re[:CHUNK_BITS], sq_im[:CHUNK_BITS], pos)
        qr, qi = _cmul(pr, pi, c_re, c_im)
        both = jnp.where(fwd, 0, pos) == 0
        right_re = jnp.where(both, qr, zero)
        right_im = jnp.where(both, qi, zero)
        strip = jnp.concatenate(
            [_dot_nt_3pass(bt_re, left_re) - _dot_nt_3pass(bt_im, left_im),
             _dot_nt_3pass(bt_re, right_re) - _dot_nt_3pass(bt_im, right_im)], axis=1)
        for s in range(t):
            start = (t - s) * SSM_GROUP
            win = pltpu.roll(strip, 2 * CHUNK_WIDTH - start, 1)[:, 0:CHUNK_WIDTH]
            m_ref[0, gi, s * SSM_GROUP:(s + 1) * SSM_GROUP, :] = win.astype(BF16)


def _ssm_prep(lam_re, lam_im, log_dt, b_re, b_im, c_re, c_im):
    depth, _, ngroups, nstate = lam_re.shape
    cat = lambda a: jnp.concatenate([a[:, 0], a[:, 1]], axis=-1)
    lr = cat(lam_re)
    li = cat(lam_im)
    ldt = cat(jnp.broadcast_to(log_dt[..., None], lam_re.shape))
    bt_re = cat(jnp.swapaxes(b_re, -1, -2))
    bt_im = cat(jnp.swapaxes(b_im, -1, -2))
    cc_re = cat(c_re)
    cc_im = cat(c_im)
    gb = min(PREP_GROUPS_PER_STEP, ngroups)
    vec = pl.BlockSpec((1, gb, 2 * nstate), lambda i, j: (i, j, 0))
    mat = pl.BlockSpec((1, gb, SSM_GROUP, 2 * nstate), lambda i, j: (i, j, 0, 0))
    big = lambda n: pl.BlockSpec((1, gb, CHUNK_WIDTH, n), lambda i, j: (i, j, 0, 0))
    return pl.pallas_call(
        _ssm_prep_kernel,
        grid=(depth, ngroups // gb),
        in_specs=[vec, vec, vec, mat, mat, mat, mat],
        out_specs=[big(2 * LANES), big(2 * LANES), big(CHUNK_WIDTH),
                   pl.BlockSpec((1, gb, 2, 2 * nstate), lambda i, j: (i, j, 0, 0))],
        out_shape=[
            jax.ShapeDtypeStruct((depth, ngroups, CHUNK_WIDTH, 2 * LANES), BF16),
            jax.ShapeDtypeStruct((depth, ngroups, CHUNK_WIDTH, 2 * LANES), BF16),
            jax.ShapeDtypeStruct((depth, ngroups, CHUNK_WIDTH, CHUNK_WIDTH), BF16),
            jax.ShapeDtypeStruct((depth, ngroups, 2, 2 * nstate), F32),
        ],
        compiler_params=_cparams("parallel", "parallel"),
        name="ssm_prep",
    )(lr, li, ldt, bt_re, bt_im, cc_re, cc_im)


def _ssm_kernel(u_ref, w_ref, m_ref, vt_ref, a_ref, d_ref, y_ref, s_scr, p_scr, *, bsz, nchunks):
    ngrp = u_ref.shape[0]
    rows = u_ref.shape[1]
    nv = nchunks // SUBLANES
    ntile = rows // SUBLANES
    sub = lax.broadcasted_iota(jnp.int32, (rows, LANES), 0) & (SUBLANES - 1)
    fwd = lax.broadcasted_iota(jnp.int32, (rows, LANES), 1) < SSM_STATE
    depth_in_tile = jnp.where(fwd, sub, SUBLANES - 1 - sub)
    fwd_t = lax.broadcasted_iota(jnp.int32, (ntile, LANES), 1) < SSM_STATE
    sub8 = lax.broadcasted_iota(jnp.int32, (SUBLANES, LANES), 0)
    fwd8 = lax.broadcasted_iota(jnp.int32, (SUBLANES, LANES), 1) < SSM_STATE

    def shifted(x, d):
        ok = depth_in_tile >= d
        moved = jnp.where(fwd, pltpu.roll(x, d, 0), pltpu.roll(x, rows - d, 0))
        return jnp.where(ok, moved, jnp.zeros_like(x))

    step8, pw = [], []
    for gi in range(ngrp):
        sq_re, sq_im = [a_ref[0, gi, 0:1, :]], [a_ref[0, gi, 1:2, :]]
        for _ in range(3):
            r2, i2 = _cmul(sq_re[-1], sq_im[-1], sq_re[-1], sq_im[-1])
            sq_re.append(r2)
            sq_im.append(i2)
        step8.append((sq_re[3], sq_im[3]))
        pw.append(_cpow_table(sq_re[:3], sq_im[:3], jnp.where(fwd8, sub8, SUBLANES - 1 - sub8)))

        s = jnp.dot(u_ref[gi].astype(BF16), w_ref[0, gi], preferred_element_type=F32)
        x_re, x_im = s[:, 0:LANES], s[:, LANES:2 * LANES]
        for bit in range(3):
            mr, mi = _cmul(sq_re[bit], sq_im[bit], shifted(x_re, 1 << bit), shifted(x_im, 1 << bit))
            x_re, x_im = x_re + mr, x_im + mi
        for part, x in enumerate((x_re, x_im)):
            s_p = s_scr.at[gi, part]
            p_p = p_scr.at[gi, part]
            s_p[...] = x
            edge = jnp.where(fwd_t, s_p[pl.ds(SUBLANES - 1, ntile, stride=SUBLANES), :],
                             s_p[pl.ds(0, ntile, stride=SUBLANES), :])
            for b in range(bsz):
                p_p[pl.ds(b, nv, stride=bsz), :] = edge[b * nv:(b + 1) * nv]
            s_p[...] = shifted(x, 1)

    def tile_step(i, carry):
        out = []
        for gi in range(ngrp):
            c = carry[gi]
            for b in range(bsz):
                rf = pl.multiple_of((b * nv + i) * SUBLANES, SUBLANES)
                rb = pl.multiple_of((b * nv + nv - 1 - i) * SUBLANES, SUBLANES)
                cb = [jnp.broadcast_to(cp[b:b + 1, :], (SUBLANES, LANES)) for cp in c]
                h = _cmul(pw[gi][0], pw[gi][1], cb[0], cb[1])
                for part in range(2):
                    e = jnp.where(fwd8, s_scr[gi, part, pl.ds(rf, SUBLANES), :],
                                  s_scr[gi, part, pl.ds(rb, SUBLANES), :])
                    hp = h[part] + e
                    s_scr[gi, part, pl.ds(rf, SUBLANES), 0:SSM_STATE] = hp[:, 0:SSM_STATE]
                    s_scr[gi, part, pl.ds(rb, SUBLANES), SSM_STATE:LANES] = hp[:, SSM_STATE:LANES]
            pf = pl.multiple_of(i * bsz, bsz)
            pb = pl.multiple_of((nv - 1 - i) * bsz, bsz)
            n = _cmul(step8[gi][0], step8[gi][1], c[0], c[1])
            out.append(tuple(
                n[part] + jnp.where(fwd8, p_scr[gi, part, pl.ds(pf, bsz), :], p_scr[gi, part, pl.ds(pb, bsz), :])
                for part in range(2)))
        return tuple(out)

    zero = jnp.zeros((bsz, LANES), F32)
    lax.fori_loop(0, nv, tile_step, tuple((zero, zero) for _ in range(ngrp)))

    for gi in range(ngrp):
        u = u_ref[gi]
        y = jnp.dot(u.astype(BF16), m_ref[0, gi], preferred_element_type=F32)
        h = jnp.concatenate([s_scr[gi, 0], s_scr[gi, 1]], axis=1).astype(BF16)
        y = y + lax.dot_general(h, vt_ref[0, gi], _DN_NT, preferred_element_type=F32)
        y_ref[gi] = y + u * d_ref[0, gi]


def _ssm_mix(uc, layer, w, m, vt, a_step, dtile, bsz):
    ngroups, rows, width = uc.shape
    nchunks = rows // bsz
    assert bsz == SUBLANES and nchunks % SUBLANES == 0
    gb = min(SSM_GROUPS_PER_STEP, ngroups)
    par = lambda shape: pl.BlockSpec((1, gb) + shape, lambda i: (layer, i) + (0,) * len(shape))
    return pl.pallas_call(
        functools.partial(_ssm_kernel, bsz=bsz, nchunks=nchunks),
        grid=(ngroups // gb,),
        in_specs=[pl.BlockSpec((gb, rows, width), lambda i: (i, 0, 0)),
                  par((width, 2 * LANES)), par((width, width)), par((width, 2 * LANES)),
                  par((2, LANES)), par((1, width))],
        out_specs=pl.BlockSpec((gb, rows, width), lambda i: (i, 0, 0)),
        out_shape=jax.ShapeDtypeStruct(uc.shape, F32),
        scratch_shapes=[pltpu.VMEM((gb, 2, rows, LANES), F32),
                        pltpu.VMEM((gb, 2, rows // SUBLANES, LANES), F32)],
        compiler_params=_cparams("parallel"),
        name="ssm_mix",
    )(uc, w, m, vt, a_step, dtile)


def _mixout_kernel(oa_ref, yc_ref, x_ref, mod_ref, wg_ref, bg_ref, an_ref, sn_ref, wo_ref, o_ref, y_scr):
    half = oa_ref.shape[2]
    m = mod_ref[0, 0]
    nk = yc_ref.shape[1]
    gran = LANES // SSM_GROUP
    cols = []
    for tile in range(half // LANES):
        y_t = y_scr.at[tile]
        for octet in range(SSM_CHUNK // gran):
            xs = [yc_ref[tile * gran + g, :, octet * LANES:(octet + 1) * LANES] for g in range(gran)]
            for s, ys in enumerate(_granule_transpose(xs)):
                y_t[pl.ds(octet * gran + s, nk, stride=CHUNK_PITCH), :] = ys
        cols.append(jnp.concatenate(
            [y_t[kc * CHUNK_PITCH:kc * CHUNK_PITCH + SSM_CHUNK, :] for kc in range(nk)], axis=0))
    y = jax.nn.gelu(jnp.concatenate(cols, axis=1))
    gate = jax.nn.sigmoid(jnp.dot(y.astype(BF16), wg_ref[0], preferred_element_type=F32) + bg_ref[0])
    o_ssm = y * gate
    na = (_rms(oa_ref[0]) * an_ref[0]).astype(BF16)
    ns = (_rms(o_ssm) * sn_ref[0]).astype(BF16)
    proj = jnp.dot(na, wo_ref[0, 0:half, :], preferred_element_type=F32)
    proj = proj + jnp.dot(ns, wo_ref[0, half:2 * half, :], preferred_element_type=F32)
    o_ref[0] = x_ref[0] + m[2:3] * proj


def _mixout(o_attn, y_chunks, x, mod6, layer, w_glu, b_glu, an, sn, w_out):
    bsz, seq, d = x.shape
    half = o_attn.shape[2]
    tm = min(TOKEN_TILE, seq)
    ntile = seq // tm
    nk = tm // SSM_CHUNK
    ngroups = y_chunks.shape[0]
    row = lambda b, i: (b, i, 0)
    return pl.pallas_call(
        _mixout_kernel,
        grid=(bsz, ntile),
        in_specs=[
            pl.BlockSpec((1, tm, half), row),
            pl.BlockSpec((ngroups, nk, CHUNK_WIDTH), lambda b, i: (0, b * ntile + i, 0)),
            pl.BlockSpec((1, tm, d), row),
            pl.BlockSpec((1, 1, N_MOD, d), lambda b, i: (layer, b, 0, 0)),
            _layer_spec((half, half), layer),
            _layer_spec((1, half), layer),
            _layer_spec((1, half), layer),
            _layer_spec((1, half), layer),
            _layer_spec((2 * half, d), layer),
        ],
        out_specs=pl.BlockSpec((1, tm, d), row),
        out_shape=jax.ShapeDtypeStruct(x.shape, F32),
        scratch_shapes=[pltpu.VMEM((half // LANES, nk * CHUNK_PITCH, LANES), F32)],
        compiler_params=_cparams("parallel", "parallel"),
        name="mixout",
    )(o_attn, y_chunks, x, mod6, w_glu, b_glu, an, sn, w_out)


def _ffn_kernel(x_ref, mod_ref, n2_ref, wi_ref, wo_ref, fn_ref, o_ref, *, final):
    hidden = wo_ref.shape[1]
    x = x_ref[0]
    m = mod_ref[0, 0]
    h = _rms(x) * n2_ref[0]
    h = h * (1.0 + m[4:5]) + m[3:4]
    gu = jnp.dot(h.astype(BF16), wi_ref[0], preferred_element_type=F32)
    gt = gu[:, 0:hidden]
    act = (gt * jax.nn.sigmoid(gt)) * gu[:, hidden:2 * hidden]
    y = x + m[5:6] * jnp.dot(act.astype(BF16), wo_ref[0], preferred_element_type=F32)
    if final:
        y = _rms(y) * fn_ref[...]
    o_ref[0] = y


def _ffn(x, mod6, layer, n2, w_in, w_out, final_norm, final):
    bsz, seq, d = x.shape
    hidden = w_out.shape[1]
    tm = min(FFN_TOKEN_TILE, seq)
    row = lambda b, i: (b, i, 0)
    return pl.pallas_call(
        functools.partial(_ffn_kernel, final=final),
        grid=(bsz, seq // tm),
        in_specs=[
            pl.BlockSpec((1, tm, d), row),
            pl.BlockSpec((1, 1, N_MOD, d), lambda b, i: (layer, b, 0, 0)),
            _layer_spec((1, d), layer),
            _layer_spec((d, 2 * hidden), layer),
            _layer_spec((hidden, d), layer),
            pl.BlockSpec((1, d), lambda b, i: (0, 0)),
        ],
        out_specs=pl.BlockSpec((1, tm, d), row),
        out_shape=jax.ShapeDtypeStruct(x.shape, F32),
        compiler_params=_cparams("parallel", "parallel"),
        name="ffn",
    )(x, mod6, n2, w_in, w_out, final_norm)


def _pair_split_perm():
    return np.concatenate([np.arange(0, HEAD_DIM, 2), np.arange(1, HEAD_DIM, 2)])


def _head_pair_order():
    rep = N_Q_HEADS // N_KV_HEADS
    return [h for t in range(rep) for h in (t, t + rep)]


def _w_in_columns(n_in):
    perm = _pair_split_perm()
    cols = [h * HEAD_DIM + perm for h in _head_pair_order()]
    k0 = N_Q_HEADS * HEAD_DIM
    cols += [k0 + g * HEAD_DIM + perm for g in range(N_KV_HEADS)]
    cols.append(np.arange(k0 + N_KV_HEADS * HEAD_DIM, n_in))
    return np.concatenate(cols)


def _rope_tables(seq):
    rows = seq // GRID_W
    row_idx = jnp.repeat(jnp.arange(rows, dtype=F32), GRID_W)
    col_idx = jnp.tile(jnp.arange(GRID_W, dtype=F32), rows)
    inv_freq = 1.0 / (ROPE_THETA ** (jnp.arange(0, HALF_HEAD, 2, dtype=F32) / HALF_HEAD))
    ang = jnp.concatenate([row_idx[:, None] * inv_freq, col_idx[:, None] * inv_freq], axis=-1)
    cos, sin = jnp.cos(ang), jnp.sin(ang)
    reps = LANES // HEAD_DIM
    cos_t = jnp.tile(jnp.concatenate([cos, cos], axis=-1), (1, reps))
    sin_t = jnp.tile(jnp.concatenate([-sin, sin], axis=-1), (1, reps))
    return cos_t, sin_t


def kernel(x, c, w_ada, b_ada, norm1, w_in, q_norm, k_norm, ssm_lam_re, ssm_lam_im, ssm_log_dt, ssm_b_re, ssm_b_im, ssm_c_re, ssm_c_im, ssm_d, w_glu, b_glu, attn_out_norm, ssm_out_norm, w_out, norm2, w_ffn_in, w_ffn_out, final_norm):
    bsz, seq, d = x.shape
    depth = w_in.shape[0]
    tm = min(TOKEN_TILE, seq)
    assert seq % tm == 0 and seq % GRID_W == 0 and tm % (SSM_CHUNK * SUBLANES) == 0 and bsz == SUBLANES

    mod6 = _modulation(c, w_ada, b_ada).reshape(depth, bsz, N_MOD, d)
    cos_t, sin_t = _rope_tables(seq)
    perm = _pair_split_perm()
    reps = LANES // HEAD_DIM
    blk = jnp.arange(LANES) // HEAD_DIM
    ones_blk = (blk[:, None] == blk[None, :]).astype(BF16)
    row3 = lambda a: a[:, None, :]

    w_in_arr = w_in[:, :, _w_in_columns(w_in.shape[2])].astype(BF16)
    qg = row3(jnp.tile(q_norm[:, perm], (1, reps)))
    kg = row3(jnp.tile(k_norm[:, perm], (1, reps)))
    attn_w = N_Q_HEADS * HEAD_DIM
    head_cols = np.concatenate([h * HEAD_DIM + np.arange(HEAD_DIM) for h in _head_pair_order()])
    an = row3(attn_out_norm[:, head_cols])
    w_out_arr = jnp.concatenate([w_out[:, head_cols, :], w_out[:, attn_w:, :]], axis=1).astype(BF16)
    w_glu_b = w_glu.astype(BF16)
    w_ffn_in_b = w_ffn_in.astype(BF16)
    w_ffn_out_b = w_ffn_out.astype(BF16)

    w_s, vt_s, m_s, a_step = _ssm_prep(ssm_lam_re, ssm_lam_im, ssm_log_dt, ssm_b_re, ssm_b_im, ssm_c_re, ssm_c_im)
    ngroups = ssm_lam_re.shape[2]
    dtile = jnp.tile(ssm_d.reshape(depth, ngroups, 1, SSM_GROUP), (1, 1, 1, SSM_CHUNK))

    for i in range(depth):
        q, k, v, u = _premix(x, mod6, i, row3(norm1), w_in_arr, qg, kg, cos_t, sin_t, ones_blk)
        o_attn = _attention(q, k, v)
        y = _ssm_mix(u, i, w_s, m_s, vt_s, a_step, dtile, bsz)
        x = _mixout(o_attn, y, x, mod6, i, w_glu_b, row3(b_glu), an, row3(ssm_out_norm), w_out_arr)
        x = _ffn(x, mod6, i, row3(norm2), w_ffn_in_b, w_ffn_out_b, final_norm.reshape(1, -1), final=(i == depth - 1))
    return x
```

```python
import functools

import numpy as np
import jax
import jax.numpy as jnp
from jax import lax
from jax.experimental import pallas as pl
from jax.experimental.pallas import tpu as pltpu

F32 = jnp.float32
BF16 = jnp.bfloat16

N_Q_HEADS = 8
N_KV_HEADS = 2
HEAD_DIM = 64
HALF_HEAD = HEAD_DIM // 2
GRID_W = 64
ROPE_THETA = 10000.0
SSM_GROUP = 16
SSM_STATE = 64
LAMBDA_RE_MAX = -1e-4
NORM_EPS = 1e-6
N_MOD = 6

LANES = 128
SUBLANES = 8
SSM_CHUNK = 32
CHUNK_BITS = SSM_CHUNK.bit_length() - 1
CHUNK_WIDTH = SSM_CHUNK * SSM_GROUP
CHUNK_PITCH = SSM_CHUNK + SUBLANES
TOKEN_TILE = 1024
SUB_TILE = 512
FFN_TOKEN_TILE = 512
ATTN_Q_BLOCK = 1024
Q_TILE = 64
ATTN_UNROLL = 4
SSM_GROUPS_PER_STEP = 4
PREP_GROUPS_PER_STEP = 8
VMEM_LIMIT = 56 * 1024 * 1024

_DN_NT = (((1,), (1,)), ((), ()))


def _cparams(*sem):
    return pltpu.CompilerParams(dimension_semantics=sem, vmem_limit_bytes=VMEM_LIMIT)


def _rms(x):
    return x * lax.rsqrt(jnp.mean(x * x, axis=-1, keepdims=True) + NORM_EPS)


def _layer_spec(shape, layer, single_buffer=False):
    index_map = lambda b, i: (layer,) + (0,) * len(shape)
    if single_buffer:
        return pl.BlockSpec((1,) + shape, index_map, pipeline_mode=pl.Buffered(1))
    return pl.BlockSpec((1,) + shape, index_map)


def _mod_kernel(c_ref, w_ref, b_ref, o_ref):
    c = c_ref[...]
    ca = (c * jax.nn.sigmoid(c)).astype(BF16)
    o_ref[0] = jnp.dot(ca, w_ref[0].astype(BF16), preferred_element_type=F32) + b_ref[0]


def _modulation(c, w_ada, b_ada):
    depth, d, n = w_ada.shape
    bsz = c.shape[0]
    tn = 1536
    return pl.pallas_call(
        _mod_kernel,
        grid=(depth, n // tn),
        in_specs=[
            pl.BlockSpec((bsz, d), lambda i, j: (0, 0)),
            pl.BlockSpec((1, d, tn), lambda i, j: (i, 0, j)),
            pl.BlockSpec((1, 1, tn), lambda i, j: (i, 0, j)),
        ],
        out_specs=pl.BlockSpec((1, bsz, tn), lambda i, j: (i, 0, j)),
        out_shape=jax.ShapeDtypeStruct((depth, bsz, n), F32),
        compiler_params=_cparams("parallel", "parallel"),
        name="adaln_mod",
    )(c, w_ada, b_ada.reshape(depth, 1, n))


def _granule_transpose(xs):
    lane = lax.broadcasted_iota(jnp.int32, xs[0].shape, 1)
    n = len(xs)
    xs = list(xs)
    for bit in (2, 1, 0):
        d = 1 << bit
        shift = d * SSM_GROUP
        hi = ((lane // SSM_GROUP) & d) != 0
        new = list(xs)
        for a in range(n):
            if a & d:
                continue
            b = a | d
            new[a] = jnp.where(hi, pltpu.roll(xs[b], shift, 1), xs[a])
            new[b] = jnp.where(hi, xs[b], pltpu.roll(xs[a], LANES - shift, 1))
        xs = new
    return xs


def _norm_rope(zt, gain, cos, sin, ones_blk):
    ssq = jnp.dot((zt * zt).astype(BF16), ones_blk, preferred_element_type=F32)
    y = zt * lax.rsqrt(ssq * (1.0 / HEAD_DIM) + NORM_EPS) * gain
    lane = lax.broadcasted_iota(jnp.int32, y.shape, 1)
    partner = jnp.where((lane & HALF_HEAD) == 0,
                        pltpu.roll(y, LANES - HALF_HEAD, 1), pltpu.roll(y, HALF_HEAD, 1))
    return y * cos + partner * sin


def _premix_kernel(x_ref, mod_ref, n1_ref, w_ref, qg_ref, kg_ref, cos_ref, sin_ref, ones_ref,
                   q_ref, k_ref, v_ref, uc_ref, u_scr):
    m = mod_ref[0, 0]
    ones_blk = ones_ref[...]
    qg = qg_ref[0] * (HEAD_DIM ** -0.5)
    nq = q_ref.shape[2]
    nsub, ntiles = u_scr.shape[0], u_scr.shape[1]
    rows = x_ref.shape[1] // nsub
    nk = rows // SSM_CHUNK
    gran = LANES // SSM_GROUP
    u0 = nq + 2 * LANES
    for sub in range(nsub):
        r = slice(sub * rows, (sub + 1) * rows)
        h = _rms(x_ref[0, r, :]) * n1_ref[0]
        h = h * (1.0 + m[1:2]) + m[0:1]
        z = jnp.dot(h.astype(BF16), w_ref[0], preferred_element_type=F32)
        cos = cos_ref[r, :]
        sin = sin_ref[r, :]
        for t in range(nq // LANES):
            zt = z[:, t * LANES:(t + 1) * LANES]
            q_ref[0, r, t * LANES:(t + 1) * LANES] = _norm_rope(zt, qg, cos, sin, ones_blk).astype(BF16)
        k_ref[0, r, :] = _norm_rope(z[:, nq:nq + LANES], kg_ref[0], cos, sin, ones_blk).astype(BF16)
        v = z[:, nq + LANES:nq + 2 * LANES]
        v_ref[0, r, 0:LANES] = v.astype(BF16)
        v_ref[0, r, LANES:2 * LANES] = jnp.ones_like(v).astype(BF16)
        for tile in range(ntiles):
            u_t = u_scr.at[sub, tile]
            for kc in range(nk):
                u_t[kc * CHUNK_PITCH:kc * CHUNK_PITCH + SSM_CHUNK, :] = (
                    z[kc * SSM_CHUNK:(kc + 1) * SSM_CHUNK, u0 + tile * LANES:u0 + (tile + 1) * LANES])
            for octet in range(SSM_CHUNK // gran):
                xs = [u_t[pl.ds(octet * gran + s, nk, stride=CHUNK_PITCH), :] for s in range(gran)]
                for g, yg in enumerate(_granule_transpose(xs)):
                    uc_ref[tile * gran + g, sub * nk:(sub + 1) * nk, octet * LANES:(octet + 1) * LANES] = yg


def _premix(x, mod6, layer, n1, w_arr, qg, kg, cos_t, sin_t, ones_blk):
    bsz, seq, d = x.shape
    nc = w_arr.shape[2]
    nq = N_Q_HEADS * HEAD_DIM
    nu = nc - nq - 2 * LANES
    tm = min(TOKEN_TILE, seq)
    ntile = seq // tm
    nk = tm // SSM_CHUNK
    nsub = max(1, tm // SUB_TILE)
    ngroups = nu // SSM_GROUP
    row = lambda b, i: (b, i, 0)
    const = lambda b, i: (0, 0)
    return pl.pallas_call(
        _premix_kernel,
        grid=(bsz, ntile),
        in_specs=[
            pl.BlockSpec((1, tm, d), row),
            pl.BlockSpec((1, 1, N_MOD, d), lambda b, i: (layer, b, 0, 0)),
            _layer_spec((1, d), layer),
            _layer_spec((d, nc), layer),
            _layer_spec((1, LANES), layer),
            _layer_spec((1, LANES), layer),
            pl.BlockSpec((tm, LANES), lambda b, i: (i, 0)),
            pl.BlockSpec((tm, LANES), lambda b, i: (i, 0)),
            pl.BlockSpec((LANES, LANES), const),
        ],
        out_specs=[
            pl.BlockSpec((1, tm, nq), row),
            pl.BlockSpec((1, tm, LANES), row),
            pl.BlockSpec((1, tm, 2 * LANES), row),
            pl.BlockSpec((ngroups, nk, CHUNK_WIDTH), lambda b, i: (0, b * ntile + i, 0)),
        ],
        out_shape=[
            jax.ShapeDtypeStruct((bsz, seq, nq), BF16),
            jax.ShapeDtypeStruct((bsz, seq, LANES), BF16),
            jax.ShapeDtypeStruct((bsz, seq, 2 * LANES), BF16),
            jax.ShapeDtypeStruct((ngroups, bsz * seq // SSM_CHUNK, CHUNK_WIDTH), F32),
        ],
        scratch_shapes=[pltpu.VMEM((nsub, nu // LANES, nk // nsub * CHUNK_PITCH, LANES), F32)],
        compiler_params=_cparams("parallel", "parallel"),
        name="premix",
    )(x, mod6, n1, w_arr, qg, kg, cos_t, sin_t, ones_blk)


def _attn_kernel(q_ref, k_ref, v_ref, o_ref, s_scr, *, tq):
    nsub = q_ref.shape[1] // tq
    rep = N_Q_HEADS // N_KV_HEADS
    lane = lax.broadcasted_iota(jnp.int32, (tq, LANES), 1)
    keep_lo = (lane < HEAD_DIM).astype(BF16)
    keep_hi = (lane >= HEAD_DIM).astype(BF16)

    def scores(j, slot):
        r0 = pl.multiple_of(j * tq, tq)
        qs = jnp.concatenate(
            [q_ref[0, pl.ds(r0, tq), (h % rep) * LANES:(h % rep + 1) * LANES] * (keep_lo if h < rep else keep_hi)
             for h in range(N_Q_HEADS)], axis=0)
        s_scr[slot] = lax.dot_general(qs, k_ref[0], _DN_NT, preferred_element_type=F32)

    def finish(j, slot):
        r0 = pl.multiple_of(j * tq, tq)
        s = s_scr[slot]
        e = jnp.exp(s - jnp.max(s, axis=-1, keepdims=True)).astype(BF16)
        pv = jnp.dot(e, v_ref[0], preferred_element_type=F32)
        o = pv[:, 0:LANES] / pv[:, LANES:2 * LANES]
        for t in range(rep):
            tile = jnp.where(lane < HEAD_DIM, o[t * tq:(t + 1) * tq], o[(t + rep) * tq:(t + rep + 1) * tq])
            o_ref[0, pl.ds(r0, tq), t * LANES:(t + 1) * LANES] = tile

    scores(0, 0)

    def run(j0, count, last):
        for t in range(count):
            if not (last and t == count - 1):
                scores(j0 + t + 1, (t + 1) % 2)
            finish(j0 + t, t % 2)

    def body(i, carry):
        run(i * ATTN_UNROLL, ATTN_UNROLL, False)
        return carry

    nloop = nsub // ATTN_UNROLL - 1
    lax.fori_loop(0, nloop, body, 0)
    run(nloop * ATTN_UNROLL, nsub - nloop * ATTN_UNROLL, True)


def _attention(q, k, v):
    bsz, seq, nq = q.shape
    tb = min(ATTN_Q_BLOCK, seq)
    tq = Q_TILE
    assert tb % (ATTN_UNROLL * tq) == 0 and ATTN_UNROLL % 2 == 0
    return pl.pallas_call(
        functools.partial(_attn_kernel, tq=tq),
        grid=(bsz, seq // tb),
        in_specs=[
            pl.BlockSpec((1, tb, nq), lambda b, i: (b, i, 0)),
            pl.BlockSpec((1, seq, LANES), lambda b, i: (b, 0, 0)),
            pl.BlockSpec((1, seq, 2 * LANES), lambda b, i: (b, 0, 0)),
        ],
        out_specs=pl.BlockSpec((1, tb, nq), lambda b, i: (b, i, 0)),
        out_shape=jax.ShapeDtypeStruct((bsz, seq, nq), F32),
        scratch_shapes=[pltpu.VMEM((2, N_Q_HEADS * tq, seq), F32)],
        compiler_params=_cparams("parallel", "parallel"),
        name="attention",
    )(q, k, v)


def _cmul(ar, ai, br, bi):
    return ar * br - ai * bi, ar * bi + ai * br


def _cpow_table(sq_re, sq_im, expo):
    pr = jnp.ones(expo.shape, F32)
    pi = jnp.zeros(expo.shape, F32)
    for bit, (sr, si) in enumerate(zip(sq_re, sq_im)):
        nr, ni = _cmul(pr, pi, sr, si)
        take = (expo & (1 << bit)) != 0
        pr = jnp.where(take, nr, pr)
        pi = jnp.where(take, ni, pi)
    return pr, pi


def _split_bf16(a):
    hi = a.astype(BF16)
    return hi, (a - hi.astype(F32)).astype(BF16)


def _dot_nt_3pass(a, b):
    ah, al = _split_bf16(a)
    bh, bl = _split_bf16(b)
    dg = functools.partial(lax.dot_general, dimension_numbers=_DN_NT, preferred_element_type=F32)
    return dg(ah, bh) + (dg(ah, bl) + dg(al, bh))


def _ssm_prep_kernel(lr_ref, li_ref, ldt_ref, br_ref, bi_ref, cr_ref, ci_ref,
                     w_ref, vt_ref, m_ref, a_ref):
    ngrp = lr_ref.shape[1]
    t = SSM_CHUNK
    shape = (CHUNK_WIDTH, LANES)
    pos = lax.broadcasted_iota(jnp.int32, shape, 0) // SSM_GROUP
    fwd = lax.broadcasted_iota(jnp.int32, shape, 1) < SSM_STATE
    for gi in range(ngrp):
        lr = jnp.minimum(lr_ref[0, gi:gi + 1, :], LAMBDA_RE_MAX)
        li = li_ref[0, gi:gi + 1, :]
        dt = jnp.exp(ldt_ref[0, gi:gi + 1, :])
        mag = jnp.exp(lr * dt)
        a_re = mag * jnp.cos(li * dt)
        a_im = mag * jnp.sin(li * dt)
        den = lr * lr + li * li
        n_re = a_re - 1.0
        k_re = (n_re * lr + a_im * li) / den
        k_im = (a_im * lr - n_re * li) / den
        br = br_ref[0, gi]
        bi = bi_ref[0, gi]
        bt_re = k_re * br - k_im * bi
        bt_im = k_re * bi + k_im * br
        bb_re = jnp.concatenate([bt_re] * t, axis=0)
        bb_im = jnp.concatenate([bt_im] * t, axis=0)
        c_re = jnp.concatenate([cr_ref[0, gi]] * t, axis=0)
        c_im = jnp.concatenate([ci_ref[0, gi]] * t, axis=0)
        sq_re, sq_im = [a_re], [a_im]
        for _ in range(CHUNK_BITS):
            r2, i2 = _cmul(sq_re[-1], sq_im[-1], sq_re[-1], sq_im[-1])
            sq_re.append(r2)
            sq_im.append(i2)
        a_ref[0, gi, 0:1, :] = sq_re[CHUNK_BITS]
        a_ref[0, gi, 1:2, :] = sq_im[CHUNK_BITS]
        up_re, up_im = _cpow_table(sq_re[:CHUNK_BITS], sq_im[:CHUNK_BITS], pos)
        flip = lambda p: jnp.concatenate(
            [p[(t - 1 - j) * SSM_GROUP:(t - j) * SSM_GROUP] for j in range(t)], axis=0)
        dn_re, dn_im = flip(up_re), flip(up_im)
        wr, wi = _cmul(jnp.where(fwd, dn_re, up_re), jnp.where(fwd, dn_im, up_im), bb_re, bb_im)
        w_ref[0, gi, :, 0:LANES] = wr.astype(BF16)
        w_ref[0, gi, :, LANES:2 * LANES] = wi.astype(BF16)
        pr, pi = _cmul(a_re, a_im, jnp.where(fwd, up_re, dn_re), jnp.where(fwd, up_im, dn_im))
        vr, vi = _cmul(pr, pi, c_re, c_im)
        vt_ref[0, gi, :, 0:LANES] = vr.astype(BF16)
        vt_ref[0, gi, :, LANES:2 * LANES] = (-vi).astype(BF16)
        zero = jnp.zeros(shape, F32)
        left_re = jnp.where(fwd, zero, vr)
        left_im = jnp.where(fwd, zero, vi)
        qr, qi = _cmul(up_re, up_im, c_re, c_im)
        both = jnp.where(fwd, 0, pos) == 0
        right_re = jnp.where(both, qr, zero)
        right_im = jnp.where(both, qi, zero)
        strip = jnp.concatenate(
            [_dot_nt_3pass(bt_re, left_re) - _dot_nt_3pass(bt_im, left_im),
             _dot_nt_3pass(bt_re, right_re) - _dot_nt_3pass(bt_im, right_im)], axis=1)
        for s in range(t):
            start = (t - s) * SSM_GROUP
            win = pltpu.roll(strip, 2 * CHUNK_WIDTH - start, 1)[:, 0:CHUNK_WIDTH]
            m_ref[0, gi, s * SSM_GROUP:(s + 1) * SSM_GROUP, :] = win.astype(BF16)


def _ssm_prep(lam_re, lam_im, log_dt, b_re, b_im, c_re, c_im):
    depth, _, ngroups, nstate = lam_re.shape
    cat = lambda a: jnp.concatenate([a[:, 0], a[:, 1]], axis=-1)
    lr = cat(lam_re)
    li = cat(lam_im)
    ldt = cat(jnp.broadcast_to(log_dt[..., None], lam_re.shape))
    bt_re = cat(jnp.swapaxes(b_re, -1, -2))
    bt_im = cat(jnp.swapaxes(b_im, -1, -2))
    cc_re = cat(c_re)
    cc_im = cat(c_im)
    gb = min(PREP_GROUPS_PER_STEP, ngroups)
    vec = pl.BlockSpec((1, gb, 2 * nstate), lambda i, j: (i, j, 0))
    mat = pl.BlockSpec((1, gb, SSM_GROUP, 2 * nstate), lambda i, j: (i, j, 0, 0))
    big = lambda n: pl.BlockSpec((1, gb, CHUNK_WIDTH, n), lambda i, j: (i, j, 0, 0))
    return pl.pallas_call(
        _ssm_prep_kernel,
        grid=(depth, ngroups // gb),
        in_specs=[vec, vec, vec, mat, mat, mat, mat],
        out_specs=[big(2 * LANES), big(2 * LANES), big(CHUNK_WIDTH),
                   pl.BlockSpec((1, gb, 2, 2 * nstate), lambda i, j: (i, j, 0, 0))],
        out_shape=[
            jax.ShapeDtypeStruct((depth, ngroups, CHUNK_WIDTH, 2 * LANES), BF16),
            jax.ShapeDtypeStruct((depth, ngroups, CHUNK_WIDTH, 2 * LANES), BF16),
            jax.ShapeDtypeStruct((depth, ngroups, CHUNK_WIDTH, CHUNK_WIDTH), BF16),
            jax.ShapeDtypeStruct((depth, ngroups, 2, 2 * nstate), F32),
        ],
        compiler_params=_cparams("parallel", "parallel"),
        name="ssm_prep",
    )(lr, li, ldt, bt_re, bt_im, cc_re, cc_im)


def _ssm_kernel(u_ref, w_ref, m_ref, vt_ref, a_ref, d_ref, y_ref, s_scr, p_scr, *, bsz, nchunks):
    ngrp = u_ref.shape[0]
    rows = u_ref.shape[1]
    nv = nchunks // SUBLANES
    ntile = rows // SUBLANES
    sub = lax.broadcasted_iota(jnp.int32, (rows, LANES), 0) & (SUBLANES - 1)
    fwd = lax.broadcasted_iota(jnp.int32, (rows, LANES), 1) < SSM_STATE
    depth_in_tile = jnp.where(fwd, sub, SUBLANES - 1 - sub)
    fwd_t = lax.broadcasted_iota(jnp.int32, (ntile, LANES), 1) < SSM_STATE
    sub8 = lax.broadcasted_iota(jnp.int32, (SUBLANES, LANES), 0)
    fwd8 = lax.broadcasted_iota(jnp.int32, (SUBLANES, LANES), 1) < SSM_STATE

    def shifted(x, d):
        ok = depth_in_tile >= d
        moved = jnp.where(fwd, pltpu.roll(x, d, 0), pltpu.roll(x, rows - d, 0))
        return jnp.where(ok, moved, jnp.zeros_like(x))

    step8, pw = [], []
    for gi in range(ngrp):
        sq_re, sq_im = [a_ref[0, gi, 0:1, :]], [a_ref[0, gi, 1:2, :]]
        for _ in range(3):
            r2, i2 = _cmul(sq_re[-1], sq_im[-1], sq_re[-1], sq_im[-1])
            sq_re.append(r2)
            sq_im.append(i2)
        step8.append((sq_re[3], sq_im[3]))
        pw.append(_cpow_table(sq_re[:3], sq_im[:3], jnp.where(fwd8, sub8, SUBLANES - 1 - sub8)))

        s = jnp.dot(u_ref[gi].astype(BF16), w_ref[0, gi], preferred_element_type=F32)
        x_re, x_im = s[:, 0:LANES], s[:, LANES:2 * LANES]
        for bit in range(3):
            mr, mi = _cmul(sq_re[bit], sq_im[bit], shifted(x_re, 1 << bit), shifted(x_im, 1 << bit))
            x_re, x_im = x_re + mr, x_im + mi
        for part, x in enumerate((x_re, x_im)):
            s_p = s_scr.at[gi, part]
            p_p = p_scr.at[gi, part]
            s_p[...] = x
            edge = jnp.where(fwd_t, s_p[pl.ds(SUBLANES - 1, ntile, stride=SUBLANES), :],
                             s_p[pl.ds(0, ntile, stride=SUBLANES), :])
            for b in range(bsz):
                p_p[pl.ds(b, nv, stride=bsz), :] = edge[b * nv:(b + 1) * nv]
            s_p[...] = shifted(x, 1)

    def tile_step(i, carry):
        out = []
        for gi in range(ngrp):
            c = carry[gi]
            for b in range(bsz):
                rf = pl.multiple_of((b * nv + i) * SUBLANES, SUBLANES)
                rb = pl.multiple_of((b * nv + nv - 1 - i) * SUBLANES, SUBLANES)
                cb = [jnp.broadcast_to(cp[b:b + 1, :], (SUBLANES, LANES)) for cp in c]
                h = _cmul(pw[gi][0], pw[gi][1], cb[0], cb[1])
                for part in range(2):
                    e = jnp.where(fwd8, s_scr[gi, part, pl.ds(rf, SUBLANES), :],
                                  s_scr[gi, part, pl.ds(rb, SUBLANES), :])
                    hp = h[part] + e
                    s_scr[gi, part, pl.ds(rf, SUBLANES), 0:SSM_STATE] = hp[:, 0:SSM_STATE]
                    s_scr[gi, part, pl.ds(rb, SUBLANES), SSM_STATE:LANES] = hp[:, SSM_STATE:LANES]
            pf = pl.multiple_of(i * bsz, bsz)
            pb = pl.multiple_of((nv - 1 - i) * bsz, bsz)
            n = _cmul(step8[gi][0], step8[gi][1], c[0], c[1])
            out.append(tuple(
                n[part] + jnp.where(fwd8, p_scr[gi, part, pl.ds(pf, bsz), :], p_scr[gi, part, pl.ds(pb, bsz), :])
                for part in range(2)))
        return tuple(out)

    zero = jnp.zeros((bsz, LANES), F32)
    lax.fori_loop(0, nv, tile_step, tuple((zero, zero) for _ in range(ngrp)))

    for gi in range(ngrp):
        u = u_ref[gi]
        y = jnp.dot(u.astype(BF16), m_ref[0, gi], preferred_element_type=F32)
        h = jnp.concatenate([s_scr[gi, 0], s_scr[gi, 1]], axis=1).astype(BF16)
        y = y + lax.dot_general(h, vt_ref[0, gi], _DN_NT, preferred_element_type=F32)
        y_ref[gi] = y + u * d_ref[0, gi]


def _ssm_mix(uc, layer, w, m, vt, a_step, dtile, bsz):
    ngroups, rows, width = uc.shape
    nchunks = rows // bsz
    assert bsz == SUBLANES and nchunks % SUBLANES == 0
    gb = min(SSM_GROUPS_PER_STEP, ngroups)
    par = lambda shape: pl.BlockSpec((1, gb) + shape, lambda i: (layer, i) + (0,) * len(shape))
    return pl.pallas_call(
        functools.partial(_ssm_kernel, bsz=bsz, nchunks=nchunks),
        grid=(ngroups // gb,),
        in_specs=[pl.BlockSpec((gb, rows, width), lambda i: (i, 0, 0)),
                  par((width, 2 * LANES)), par((width, width)), par((width, 2 * LANES)),
                  par((2, LANES)), par((1, width))],
        out_specs=pl.BlockSpec((gb, rows, width), lambda i: (i, 0, 0)),
        out_shape=jax.ShapeDtypeStruct(uc.shape, F32),
        scratch_shapes=[pltpu.VMEM((gb, 2, rows, LANES), F32),
                        pltpu.VMEM((gb, 2, rows // SUBLANES, LANES), F32)],
        compiler_params=_cparams("parallel"),
        name="ssm_mix",
    )(uc, w, m, vt, a_step, dtile)


def _mixout_kernel(oa_ref, yc_ref, x_ref, mod_ref, wg_ref, bg_ref, an_ref, sn_ref, wo_ref, o_ref, y_scr):
    half = oa_ref.shape[2]
    m = mod_ref[0, 0]
    nsub = y_scr.shape[0]
    rows = x_ref.shape[1] // nsub
    nk = rows // SSM_CHUNK
    gran = LANES // SSM_GROUP
    for sub in range(nsub):
        r = slice(sub * rows, (sub + 1) * rows)
        cols = []
        for tile in range(half // LANES):
            y_t = y_scr.at[sub, tile]
            for octet in range(SSM_CHUNK // gran):
                xs = [yc_ref[tile * gran + g, sub * nk:(sub + 1) * nk, octet * LANES:(octet + 1) * LANES]
                      for g in range(gran)]
                for s, ys in enumerate(_granule_transpose(xs)):
                    y_t[pl.ds(octet * gran + s, nk, stride=CHUNK_PITCH), :] = ys
            cols.append(jnp.concatenate(
                [y_t[kc * CHUNK_PITCH:kc * CHUNK_PITCH + SSM_CHUNK, :] for kc in range(nk)], axis=0))
        y = jax.nn.gelu(jnp.concatenate(cols, axis=1))
        gate = jax.nn.sigmoid(jnp.dot(y.astype(BF16), wg_ref[0], preferred_element_type=F32) + bg_ref[0])
        o_ssm = y * gate
        na = (_rms(oa_ref[0, r, :]) * an_ref[0]).astype(BF16)
        ns = (_rms(o_ssm) * sn_ref[0]).astype(BF16)
        proj = jnp.dot(na, wo_ref[0, 0:half, :], preferred_element_type=F32)
        proj = proj + jnp.dot(ns, wo_ref[0, half:2 * half, :], preferred_element_type=F32)
        o_ref[0, r, :] = x_ref[0, r, :] + m[2:3] * proj


def _mixout(o_attn, y_chunks, x, mod6, layer, w_glu, b_glu, an, sn, w_out):
    bsz, seq, d = x.shape
    half = o_attn.shape[2]
    tm = min(TOKEN_TILE, seq)
    ntile = seq // tm
    nk = tm // SSM_CHUNK
    nsub = max(1, tm // SUB_TILE)
    ngroups = y_chunks.shape[0]
    row = lambda b, i: (b, i, 0)
    return pl.pallas_call(
        _mixout_kernel,
        grid=(bsz, ntile),
        in_specs=[
            pl.BlockSpec((1, tm, half), row),
            pl.BlockSpec((ngroups, nk, CHUNK_WIDTH), lambda b, i: (0, b * ntile + i, 0)),
            pl.BlockSpec((1, tm, d), row),
            pl.BlockSpec((1, 1, N_MOD, d), lambda b, i: (layer, b, 0, 0)),
            _layer_spec((half, half), layer),
            _layer_spec((1, half), layer),
            _layer_spec((1, half), layer),
            _layer_spec((1, half), layer),
            _layer_spec((2 * half, d), layer),
        ],
        out_specs=pl.BlockSpec((1, tm, d), row),
        out_shape=jax.ShapeDtypeStruct(x.shape, F32),
        scratch_shapes=[pltpu.VMEM((nsub, half // LANES, nk // nsub * CHUNK_PITCH, LANES), F32)],
        compiler_params=_cparams("parallel", "parallel"),
        name="mixout",
    )(o_attn, y_chunks, x, mod6, w_glu, b_glu, an, sn, w_out)


def _ffn_kernel(x_ref, mod_ref, n2_ref, wi_ref, wo_ref, fn_ref, o_ref, *, final):
    hidden = wo_ref.shape[1]
    x = x_ref[0]
    m = mod_ref[0, 0]
    h = _rms(x) * n2_ref[0]
    h = h * (1.0 + m[4:5]) + m[3:4]
    gu = jnp.dot(h.astype(BF16), wi_ref[0], preferred_element_type=F32)
    gt = gu[:, 0:hidden]
    act = (gt * jax.nn.sigmoid(gt)) * gu[:, hidden:2 * hidden]
    y = x + m[5:6] * jnp.dot(act.astype(BF16), wo_ref[0], preferred_element_type=F32)
    if final:
        y = _rms(y) * fn_ref[...]
    o_ref[0] = y


def _ffn(x, mod6, layer, n2, w_in, w_out, final_norm, final):
    bsz, seq, d = x.shape
    hidden = w_out.shape[1]
    tm = min(FFN_TOKEN_TILE, seq)
    row = lambda b, i: (b, i, 0)
    return pl.pallas_call(
        functools.partial(_ffn_kernel, final=final),
        grid=(bsz, seq // tm),
        in_specs=[
            pl.BlockSpec((1, tm, d), row),
            pl.BlockSpec((1, 1, N_MOD, d), lambda b, i: (layer, b, 0, 0)),
            _layer_spec((1, d), layer),
            _layer_spec((d, 2 * hidden), layer, single_buffer=True),
            _layer_spec((hidden, d), layer, single_buffer=True),
            pl.BlockSpec((1, d), lambda b, i: (0, 0)),
        ],
        out_specs=pl.BlockSpec((1, tm, d), row),
        out_shape=jax.ShapeDtypeStruct(x.shape, F32),
        compiler_params=_cparams("parallel", "parallel"),
        name="ffn",
    )(x, mod6, n2, w_in, w_out, final_norm)


def _pair_split_perm():
    return np.concatenate([np.arange(0, HEAD_DIM, 2), np.arange(1, HEAD_DIM, 2)])


def _head_pair_order():
    rep = N_Q_HEADS // N_KV_HEADS
    return [h for t in range(rep) for h in (t, t + rep)]


def _w_in_columns(n_in):
    perm = _pair_split_perm()
    cols = [h * HEAD_DIM + perm for h in _head_pair_order()]
    k0 = N_Q_HEADS * HEAD_DIM
    cols += [k0 + g * HEAD_DIM + perm for g in range(N_KV_HEADS)]
    cols.append(np.arange(k0 + N_KV_HEADS * HEAD_DIM, n_in))
    return np.concatenate(cols)


def _rope_tables(seq):
    rows = seq // GRID_W
    row_idx = jnp.repeat(jnp.arange(rows, dtype=F32), GRID_W)
    col_idx = jnp.tile(jnp.arange(GRID_W, dtype=F32), rows)
    inv_freq = 1.0 / (ROPE_THETA ** (jnp.arange(0, HALF_HEAD, 2, dtype=F32) / HALF_HEAD))
    ang = jnp.concatenate([row_idx[:, None] * inv_freq, col_idx[:, None] * inv_freq], axis=-1)
    cos, sin = jnp.cos(ang), jnp.sin(ang)
    reps = LANES // HEAD_DIM
    cos_t = jnp.tile(jnp.concatenate([cos, cos], axis=-1), (1, reps))
    sin_t = jnp.tile(jnp.concatenate([-sin, sin], axis=-1), (1, reps))
    return cos_t, sin_t


def kernel(x, c, w_ada, b_ada, norm1, w_in, q_norm, k_norm, ssm_lam_re, ssm_lam_im, ssm_log_dt, ssm_b_re, ssm_b_im, ssm_c_re, ssm_c_im, ssm_d, w_glu, b_glu, attn_out_norm, ssm_out_norm, w_out, norm2, w_ffn_in, w_ffn_out, final_norm):
    bsz, seq, d = x.shape
    depth = w_in.shape[0]
    tm = min(TOKEN_TILE, seq)
    assert seq % tm == 0 and seq % GRID_W == 0 and tm % (SSM_CHUNK * SUBLANES) == 0 and bsz == SUBLANES

    mod6 = _modulation(c, w_ada, b_ada).reshape(depth, bsz, N_MOD, d)
    cos_t, sin_t = _rope_tables(seq)
    perm = _pair_split_perm()
    reps = LANES // HEAD_DIM
    blk = jnp.arange(LANES) // HEAD_DIM
    ones_blk = (blk[:, None] == blk[None, :]).astype(BF16)
    row3 = lambda a: a[:, None, :]

    w_in_arr = w_in[:, :, _w_in_columns(w_in.shape[2])].astype(BF16)
    qg = row3(jnp.tile(q_norm[:, perm], (1, reps)))
    kg = row3(jnp.tile(k_norm[:, perm], (1, reps)))
    attn_w = N_Q_HEADS * HEAD_DIM
    head_cols = np.concatenate([h * HEAD_DIM + np.arange(HEAD_DIM) for h in _head_pair_order()])
    an = row3(attn_out_norm[:, head_cols])
    w_out_arr = jnp.concatenate([w_out[:, head_cols, :], w_out[:, attn_w:, :]], axis=1).astype(BF16)
    w_glu_b = w_glu.astype(BF16)
    w_ffn_in_b = w_ffn_in.astype(BF16)
    w_ffn_out_b = w_ffn_out.astype(BF16)

    w_s, vt_s, m_s, a_step = _ssm_prep(ssm_lam_re, ssm_lam_im, ssm_log_dt, ssm_b_re, ssm_b_im, ssm_c_re, ssm_c_im)
    ngroups = ssm_lam_re.shape[2]
    dtile = jnp.tile(ssm_d.reshape(depth, ngroups, 1, SSM_GROUP), (1, 1, 1, SSM_CHUNK))

    for i in range(depth):
        q, k, v, u = _premix(x, mod6, i, row3(norm1), w_in_arr, qg, kg, cos_t, sin_t, ones_blk)
        o_attn = _attention(q, k, v)
        y = _ssm_mix(u, i, w_s, m_s, vt_s, a_step, dtile, bsz)
        x = _mixout(o_attn, y, x, mod6, i, w_glu_b, row3(b_glu), an, row3(ssm_out_norm), w_out_arr)
        x = _ffn(x, mod6, i, row3(norm2), w_ffn_in_b, w_ffn_out_b, final_norm.reshape(1, -1), final=(i == depth - 1))
    return x
```

```python
import functools

import numpy as np
import jax
import jax.numpy as jnp
from jax import lax
from jax.experimental import pallas as pl
from jax.experimental.pallas import tpu as pltpu

F32 = jnp.float32
BF16 = jnp.bfloat16

N_Q_HEADS = 8
N_KV_HEADS = 2
HEAD_DIM = 64
HALF_HEAD = HEAD_DIM // 2
GRID_W = 64
ROPE_THETA = 10000.0
SSM_GROUP = 16
SSM_STATE = 64
LAMBDA_RE_MAX = -1e-4
NORM_EPS = 1e-6
N_MOD = 6

LANES = 128
SUBLANES = 8
SSM_CHUNK = 32
CHUNK_BITS = SSM_CHUNK.bit_length() - 1
CHUNK_WIDTH = SSM_CHUNK * SSM_GROUP
CHUNK_PITCH = SSM_CHUNK + SUBLANES
TOKEN_TILE = 1024
SUB_TILE = 512
FFN_TOKEN_TILE = 512
ATTN_Q_BLOCK = 1024
Q_TILE = 64
ATTN_UNROLL = 4
SSM_GROUPS_PER_STEP = 4
PREP_GROUPS_PER_STEP = 8
VMEM_LIMIT = 56 * 1024 * 1024

_DN_NT = (((1,), (1,)), ((), ()))


def _cparams(*sem):
    return pltpu.CompilerParams(dimension_semantics=sem, vmem_limit_bytes=VMEM_LIMIT)


def _rms(x):
    return x * lax.rsqrt(jnp.mean(x * x, axis=-1, keepdims=True) + NORM_EPS)


def _layer_spec(shape, layer, single_buffer=False):
    index_map = lambda b, i: (layer,) + (0,) * len(shape)
    if single_buffer:
        return pl.BlockSpec((1,) + shape, index_map, pipeline_mode=pl.Buffered(1))
    return pl.BlockSpec((1,) + shape, index_map)


def _mod_kernel(c_ref, w_ref, b_ref, o_ref):
    c = c_ref[...]
    ca = (c * jax.nn.sigmoid(c)).astype(BF16)
    o_ref[0] = jnp.dot(ca, w_ref[0].astype(BF16), preferred_element_type=F32) + b_ref[0]


def _modulation(c, w_ada, b_ada):
    depth, d, n = w_ada.shape
    bsz = c.shape[0]
    tn = 1536
    return pl.pallas_call(
        _mod_kernel,
        grid=(depth, n // tn),
        in_specs=[
            pl.BlockSpec((bsz, d), lambda i, j: (0, 0)),
            pl.BlockSpec((1, d, tn), lambda i, j: (i, 0, j)),
            pl.BlockSpec((1, 1, tn), lambda i, j: (i, 0, j)),
        ],
        out_specs=pl.BlockSpec((1, bsz, tn), lambda i, j: (i, 0, j)),
        out_shape=jax.ShapeDtypeStruct((depth, bsz, n), F32),
        compiler_params=_cparams("parallel", "parallel"),
        name="adaln_mod",
    )(c, w_ada, b_ada.reshape(depth, 1, n))


def _granule_transpose(xs):
    lane = lax.broadcasted_iota(jnp.int32, xs[0].shape, 1)
    n = len(xs)
    xs = list(xs)
    for bit in (2, 1, 0):
        d = 1 << bit
        shift = d * SSM_GROUP
        hi = ((lane // SSM_GROUP) & d) != 0
        new = list(xs)
        for a in range(n):
            if a & d:
                continue
            b = a | d
            new[a] = jnp.where(hi, pltpu.roll(xs[b], shift, 1), xs[a])
            new[b] = jnp.where(hi, xs[b], pltpu.roll(xs[a], LANES - shift, 1))
        xs = new
    return xs


def _norm_rope(zt, gain, cos, sin, ones_blk):
    ssq = jnp.dot((zt * zt).astype(BF16), ones_blk, preferred_element_type=F32)
    y = zt * lax.rsqrt(ssq * (1.0 / HEAD_DIM) + NORM_EPS) * gain
    partner = pltpu.roll(y, LANES // 2, 1)
    return y * cos + partner * sin


def _premix_kernel(x_ref, mod_ref, n1_ref, w_ref, qg_ref, kg_ref, cos_ref, sin_ref, ones_ref,
                   q_ref, k_ref, v_ref, uc_ref, u_scr):
    m = mod_ref[0, 0]
    ones_blk = ones_ref[...]
    qg = qg_ref[0] * (HEAD_DIM ** -0.5)
    nq = q_ref.shape[2]
    nsub, ntiles = u_scr.shape[0], u_scr.shape[1]
    rows = x_ref.shape[1] // nsub
    nk = rows // SSM_CHUNK
    gran = LANES // SSM_GROUP
    u0 = nq + 2 * LANES
    for sub in range(nsub):
        r = slice(sub * rows, (sub + 1) * rows)
        h = _rms(x_ref[0, r, :]) * n1_ref[0]
        h = h * (1.0 + m[1:2]) + m[0:1]
        z = jnp.dot(h.astype(BF16), w_ref[0], preferred_element_type=F32)
        cos = cos_ref[r, :]
        sin = sin_ref[r, :]
        for t in range(nq // LANES):
            zt = z[:, t * LANES:(t + 1) * LANES]
            q_ref[0, r, t * LANES:(t + 1) * LANES] = _norm_rope(zt, qg, cos, sin, ones_blk).astype(BF16)
        k_ref[0, r, :] = _norm_rope(z[:, nq:nq + LANES], kg_ref[0], cos, sin, ones_blk).astype(BF16)
        v = z[:, nq + LANES:nq + 2 * LANES]
        v_ref[0, r, 0:LANES] = v.astype(BF16)
        v_ref[0, r, LANES:2 * LANES] = jnp.ones_like(v).astype(BF16)
        for tile in range(ntiles):
            u_t = u_scr.at[sub, tile]
            for kc in range(nk):
                u_t[kc * CHUNK_PITCH:kc * CHUNK_PITCH + SSM_CHUNK, :] = (
                    z[kc * SSM_CHUNK:(kc + 1) * SSM_CHUNK, u0 + tile * LANES:u0 + (tile + 1) * LANES])
            for octet in range(SSM_CHUNK // gran):
                xs = [u_t[pl.ds(octet * gran + s, nk, stride=CHUNK_PITCH), :] for s in range(gran)]
                for g, yg in enumerate(_granule_transpose(xs)):
                    uc_ref[tile * gran + g, sub * nk:(sub + 1) * nk, octet * LANES:(octet + 1) * LANES] = yg


def _premix(x, mod6, layer, n1, w_arr, qg, kg, cos_t, sin_t, ones_blk):
    bsz, seq, d = x.shape
    nc = w_arr.shape[2]
    nq = N_Q_HEADS * HEAD_DIM
    nu = nc - nq - 2 * LANES
    tm = min(TOKEN_TILE, seq)
    ntile = seq // tm
    nk = tm // SSM_CHUNK
    nsub = max(1, tm // SUB_TILE)
    ngroups = nu // SSM_GROUP
    row = lambda b, i: (b, i, 0)
    const = lambda b, i: (0, 0)
    return pl.pallas_call(
        _premix_kernel,
        grid=(bsz, ntile),
        in_specs=[
            pl.BlockSpec((1, tm, d), row),
            pl.BlockSpec((1, 1, N_MOD, d), lambda b, i: (layer, b, 0, 0)),
            _layer_spec((1, d), layer),
            _layer_spec((d, nc), layer),
            _layer_spec((1, LANES), layer),
            _layer_spec((1, LANES), layer),
            pl.BlockSpec((tm, LANES), lambda b, i: (i, 0)),
            pl.BlockSpec((tm, LANES), lambda b, i: (i, 0)),
            pl.BlockSpec((LANES, LANES), const),
        ],
        out_specs=[
            pl.BlockSpec((1, tm, nq), row),
            pl.BlockSpec((1, tm, LANES), row),
            pl.BlockSpec((1, tm, 2 * LANES), row),
            pl.BlockSpec((ngroups, nk, CHUNK_WIDTH), lambda b, i: (0, b * ntile + i, 0)),
        ],
        out_shape=[
            jax.ShapeDtypeStruct((bsz, seq, nq), BF16),
            jax.ShapeDtypeStruct((bsz, seq, LANES), BF16),
            jax.ShapeDtypeStruct((bsz, seq, 2 * LANES), BF16),
            jax.ShapeDtypeStruct((ngroups, bsz * seq // SSM_CHUNK, CHUNK_WIDTH), F32),
        ],
        scratch_shapes=[pltpu.VMEM((nsub, nu // LANES, nk // nsub * CHUNK_PITCH, LANES), F32)],
        compiler_params=_cparams("parallel", "parallel"),
        name="premix",
    )(x, mod6, n1, w_arr, qg, kg, cos_t, sin_t, ones_blk)


def _attn_kernel(q_ref, k_ref, v_ref, o_ref, s_scr, *, tq):
    nsub = q_ref.shape[1] // tq
    rep = N_Q_HEADS // N_KV_HEADS
    lane = lax.broadcasted_iota(jnp.int32, (tq, LANES), 1)
    which = (lane // HALF_HEAD) & 1
    keep_lo = (which == 0).astype(BF16)
    keep_hi = (which == 1).astype(BF16)

    def scores(j, slot):
        r0 = pl.multiple_of(j * tq, tq)
        qs = jnp.concatenate(
            [q_ref[0, pl.ds(r0, tq), (h % rep) * LANES:(h % rep + 1) * LANES] * (keep_lo if h < rep else keep_hi)
             for h in range(N_Q_HEADS)], axis=0)
        s_scr[slot] = lax.dot_general(qs, k_ref[0], _DN_NT, preferred_element_type=F32)

    def finish(j, slot):
        r0 = pl.multiple_of(j * tq, tq)
        s = s_scr[slot]
        e = jnp.exp(s - jnp.max(s, axis=-1, keepdims=True)).astype(BF16)
        pv = jnp.dot(e, v_ref[0], preferred_element_type=F32)
        o = pv[:, 0:LANES] / pv[:, LANES:2 * LANES]
        for t in range(rep):
            tile = jnp.where(lane < HEAD_DIM, o[t * tq:(t + 1) * tq], o[(t + rep) * tq:(t + rep + 1) * tq])
            o_ref[0, pl.ds(r0, tq), t * LANES:(t + 1) * LANES] = tile

    scores(0, 0)

    def run(j0, count, last):
        for t in range(count):
            if not (last and t == count - 1):
                scores(j0 + t + 1, (t + 1) % 2)
            finish(j0 + t, t % 2)

    def body(i, carry):
        run(i * ATTN_UNROLL, ATTN_UNROLL, False)
        return carry

    nloop = nsub // ATTN_UNROLL - 1
    lax.fori_loop(0, nloop, body, 0)
    run(nloop * ATTN_UNROLL, nsub - nloop * ATTN_UNROLL, True)


def _attention(q, k, v):
    bsz, seq, nq = q.shape
    tb = min(ATTN_Q_BLOCK, seq)
    tq = Q_TILE
    assert tb % (ATTN_UNROLL * tq) == 0 and ATTN_UNROLL % 2 == 0
    return pl.pallas_call(
        functools.partial(_attn_kernel, tq=tq),
        grid=(bsz, seq // tb),
        in_specs=[
            pl.BlockSpec((1, tb, nq), lambda b, i: (b, i, 0)),
            pl.BlockSpec((1, seq, LANES), lambda b, i: (b, 0, 0)),
            pl.BlockSpec((1, seq, 2 * LANES), lambda b, i: (b, 0, 0)),
        ],
        out_specs=pl.BlockSpec((1, tb, nq), lambda b, i: (b, i, 0)),
        out_shape=jax.ShapeDtypeStruct((bsz, seq, nq), F32),
        scratch_shapes=[pltpu.VMEM((2, N_Q_HEADS * tq, seq), F32)],
        compiler_params=_cparams("parallel", "parallel"),
        name="attention",
    )(q, k, v)


def _cmul(ar, ai, br, bi):
    return ar * br - ai * bi, ar * bi + ai * br


def _cpow_table(sq_re, sq_im, expo):
    pr = jnp.ones(expo.shape, F32)
    pi = jnp.zeros(expo.shape, F32)
    for bit, (sr, si) in enumerate(zip(sq_re, sq_im)):
        nr, ni = _cmul(pr, pi, sr, si)
        take = (expo & (1 << bit)) != 0
        pr = jnp.where(take, nr, pr)
        pi = jnp.where(take, ni, pi)
    return pr, pi


def _split_bf16(a):
    hi = a.astype(BF16)
    return hi, (a - hi.astype(F32)).astype(BF16)


def _dot_nt_3pass(a, b):
    ah, al = _split_bf16(a)
    bh, bl = _split_bf16(b)
    dg = functools.partial(lax.dot_general, dimension_numbers=_DN_NT, preferred_element_type=F32)
    return dg(ah, bh) + (dg(ah, bl) + dg(al, bh))


def _ssm_prep_kernel(lr_ref, li_ref, ldt_ref, br_ref, bi_ref, cr_ref, ci_ref,
                     w_ref, vt_ref, m_ref, a_ref):
    ngrp = lr_ref.shape[1]
    t = SSM_CHUNK
    shape = (CHUNK_WIDTH, LANES)
    pos = lax.broadcasted_iota(jnp.int32, shape, 0) // SSM_GROUP
    fwd = lax.broadcasted_iota(jnp.int32, shape, 1) < SSM_STATE
    for gi in range(ngrp):
        lr = jnp.minimum(lr_ref[0, gi:gi + 1, :], LAMBDA_RE_MAX)
        li = li_ref[0, gi:gi + 1, :]
        dt = jnp.exp(ldt_ref[0, gi:gi + 1, :])
        mag = jnp.exp(lr * dt)
        a_re = mag * jnp.cos(li * dt)
        a_im = mag * jnp.sin(li * dt)
        den = lr * lr + li * li
        n_re = a_re - 1.0
        k_re = (n_re * lr + a_im * li) / den
        k_im = (a_im * lr - n_re * li) / den
        br = br_ref[0, gi]
        bi = bi_ref[0, gi]
        bt_re = k_re * br - k_im * bi
        bt_im = k_re * bi + k_im * br
        bb_re = jnp.concatenate([bt_re] * t, axis=0)
        bb_im = jnp.concatenate([bt_im] * t, axis=0)
        c_re = jnp.concatenate([cr_ref[0, gi]] * t, axis=0)
        c_im = jnp.concatenate([ci_ref[0, gi]] * t, axis=0)
        sq_re, sq_im = [a_re], [a_im]
        for _ in range(CHUNK_BITS):
            r2, i2 = _cmul(sq_re[-1], sq_im[-1], sq_re[-1], sq_im[-1])
            sq_re.append(r2)
            sq_im.append(i2)
        a_ref[0, gi, 0:1, :] = sq_re[CHUNK_BITS]
        a_ref[0, gi, 1:2, :] = sq_im[CHUNK_BITS]
        up_re, up_im = _cpow_table(sq_re[:CHUNK_BITS], sq_im[:CHUNK_BITS], pos)
        flip = lambda p: jnp.concatenate(
            [p[(t - 1 - j) * SSM_GROUP:(t - j) * SSM_GROUP] for j in range(t)], axis=0)
        dn_re, dn_im = flip(up_re), flip(up_im)
        wr, wi = _cmul(jnp.where(fwd, dn_re, up_re), jnp.where(fwd, dn_im, up_im), bb_re, bb_im)
        w_ref[0, gi, :, 0:LANES] = wr.astype(BF16)
        w_ref[0, gi, :, LANES:2 * LANES] = wi.astype(BF16)
        pr, pi = _cmul(a_re, a_im, jnp.where(fwd, up_re, dn_re), jnp.where(fwd, up_im, dn_im))
        vr, vi = _cmul(pr, pi, c_re, c_im)
        vt_ref[0, gi, :, 0:LANES] = vr.astype(BF16)
        vt_ref[0, gi, :, LANES:2 * LANES] = (-vi).astype(BF16)
        zero = jnp.zeros(shape, F32)
        left_re = jnp.where(fwd, zero, vr)
        left_im = jnp.where(fwd, zero, vi)
        qr, qi = _cmul(up_re, up_im, c_re, c_im)
        both = jnp.where(fwd, 0, pos) == 0
        right_re = jnp.where(both, qr, zero)
        right_im = jnp.where(both, qi, zero)
        strip = jnp.concatenate(
            [_dot_nt_3pass(bt_re, left_re) - _dot_nt_3pass(bt_im, left_im),
             _dot_nt_3pass(bt_re, right_re) - _dot_nt_3pass(bt_im, right_im)], axis=1)
        for s in range(t):
            start = (t - s) * SSM_GROUP
            win = pltpu.roll(strip, 2 * CHUNK_WIDTH - start, 1)[:, 0:CHUNK_WIDTH]
            m_ref[0, gi, s * SSM_GROUP:(s + 1) * SSM_GROUP, :] = win.astype(BF16)


def _ssm_prep(lam_re, lam_im, log_dt, b_re, b_im, c_re, c_im):
    depth, _, ngroups, nstate = lam_re.shape
    cat = lambda a: jnp.concatenate([a[:, 0], a[:, 1]], axis=-1)
    lr = cat(lam_re)
    li = cat(lam_im)
    ldt = cat(jnp.broadcast_to(log_dt[..., None], lam_re.shape))
    bt_re = cat(jnp.swapaxes(b_re, -1, -2))
    bt_im = cat(jnp.swapaxes(b_im, -1, -2))
    cc_re = cat(c_re)
    cc_im = cat(c_im)
    gb = min(PREP_GROUPS_PER_STEP, ngroups)
    vec = pl.BlockSpec((1, gb, 2 * nstate), lambda i, j: (i, j, 0))
    mat = pl.BlockSpec((1, gb, SSM_GROUP, 2 * nstate), lambda i, j: (i, j, 0, 0))
    big = lambda n: pl.BlockSpec((1, gb, CHUNK_WIDTH, n), lambda i, j: (i, j, 0, 0))
    return pl.pallas_call(
        _ssm_prep_kernel,
        grid=(depth, ngroups // gb),
        in_specs=[vec, vec, vec, mat, mat, mat, mat],
        out_specs=[big(2 * LANES), big(2 * LANES), big(CHUNK_WIDTH),
                   pl.BlockSpec((1, gb, 2, 2 * nstate), lambda i, j: (i, j, 0, 0))],
        out_shape=[
            jax.ShapeDtypeStruct((depth, ngroups, CHUNK_WIDTH, 2 * LANES), BF16),
            jax.ShapeDtypeStruct((depth, ngroups, CHUNK_WIDTH, 2 * LANES), BF16),
            jax.ShapeDtypeStruct((depth, ngroups, CHUNK_WIDTH, CHUNK_WIDTH), BF16),
            jax.ShapeDtypeStruct((depth, ngroups, 2, 2 * nstate), F32),
        ],
        compiler_params=_cparams("parallel", "parallel"),
        name="ssm_prep",
    )(lr, li, ldt, bt_re, bt_im, cc_re, cc_im)


def _ssm_kernel(u_ref, w_ref, m_ref, vt_ref, a_ref, d_ref, y_ref, s_scr, p_scr, *, bsz, nchunks):
    ngrp = u_ref.shape[0]
    rows = u_ref.shape[1]
    nv = nchunks // SUBLANES
    ntile = rows // SUBLANES
    tiled = (ntile, SUBLANES, LANES)
    sub = lax.broadcasted_iota(jnp.int32, tiled, 1)
    fwd = lax.broadcasted_iota(jnp.int32, tiled, 2) < SSM_STATE
    depth_in_tile = jnp.where(fwd, sub, SUBLANES - 1 - sub)
    fwd_t = lax.broadcasted_iota(jnp.int32, (ntile, LANES), 1) < SSM_STATE
    sub8 = lax.broadcasted_iota(jnp.int32, (SUBLANES, LANES), 0)
    fwd8 = lax.broadcasted_iota(jnp.int32, (SUBLANES, LANES), 1) < SSM_STATE

    def shifted(x, d):
        ok = depth_in_tile >= d
        moved = jnp.where(fwd, pltpu.roll(x, d, 1), pltpu.roll(x, SUBLANES - d, 1))
        return jnp.where(ok, moved, jnp.zeros_like(x))

    step8, pw = [], []
    for gi in range(ngrp):
        sq_re, sq_im = [a_ref[0, gi, 0:1, :]], [a_ref[0, gi, 1:2, :]]
        for _ in range(3):
            r2, i2 = _cmul(sq_re[-1], sq_im[-1], sq_re[-1], sq_im[-1])
            sq_re.append(r2)
            sq_im.append(i2)
        step8.append((sq_re[3], sq_im[3]))
        pw.append(_cpow_table(sq_re[:3], sq_im[:3], jnp.where(fwd8, sub8, SUBLANES - 1 - sub8)))

        s = jnp.dot(u_ref[gi].astype(BF16), w_ref[0, gi], preferred_element_type=F32)
        x_re, x_im = s[:, 0:LANES].reshape(tiled), s[:, LANES:2 * LANES].reshape(tiled)
        for bit in range(3):
            mr, mi = _cmul(sq_re[bit], sq_im[bit], shifted(x_re, 1 << bit), shifted(x_im, 1 << bit))
            x_re, x_im = x_re + mr, x_im + mi
        for part, x in enumerate((x_re, x_im)):
            s_p = s_scr.at[gi, part]
            p_p = p_scr.at[gi, part]
            s_p[...] = x.reshape(rows, LANES)
            edge = jnp.where(fwd_t, s_p[pl.ds(SUBLANES - 1, ntile, stride=SUBLANES), :],
                             s_p[pl.ds(0, ntile, stride=SUBLANES), :])
            for b in range(bsz):
                p_p[pl.ds(b, nv, stride=bsz), :] = edge[b * nv:(b + 1) * nv]
            s_p[...] = shifted(x, 1).reshape(rows, LANES)

    def tile_step(i, carry):
        out = []
        for gi in range(ngrp):
            c = carry[gi]
            rf = [pl.multiple_of((b * nv + i) * SUBLANES, SUBLANES) for b in range(bsz)]
            rb = [pl.multiple_of((b * nv + nv - 1 - i) * SUBLANES, SUBLANES) for b in range(bsz)]
            e = [[jnp.where(fwd8, s_scr[gi, part, pl.ds(rf[b], SUBLANES), :],
                            s_scr[gi, part, pl.ds(rb[b], SUBLANES), :]) for part in range(2)]
                 for b in range(bsz)]
            for b in range(bsz):
                cb = [jnp.broadcast_to(cp[b:b + 1, :], (SUBLANES, LANES)) for cp in c]
                h = _cmul(pw[gi][0], pw[gi][1], cb[0], cb[1])
                for part in range(2):
                    hp = h[part] + e[b][part]
                    s_scr[gi, part, pl.ds(rf[b], SUBLANES), 0:SSM_STATE] = hp[:, 0:SSM_STATE]
                    s_scr[gi, part, pl.ds(rb[b], SUBLANES), SSM_STATE:LANES] = hp[:, SSM_STATE:LANES]
            pf = pl.multiple_of(i * bsz, bsz)
            pb = pl.multiple_of((nv - 1 - i) * bsz, bsz)
            n = _cmul(step8[gi][0], step8[gi][1], c[0], c[1])
            out.append(tuple(
                n[part] + jnp.where(fwd8, p_scr[gi, part, pl.ds(pf, bsz), :], p_scr[gi, part, pl.ds(pb, bsz), :])
                for part in range(2)))
        return tuple(out)

    zero = jnp.zeros((bsz, LANES), F32)
    lax.fori_loop(0, nv, tile_step, tuple((zero, zero) for _ in range(ngrp)))

    for gi in range(ngrp):
        u = u_ref[gi]
        y = jnp.dot(u.astype(BF16), m_ref[0, gi], preferred_element_type=F32)
        h = jnp.concatenate([s_scr[gi, 0], s_scr[gi, 1]], axis=1).astype(BF16)
        y = y + lax.dot_general(h, vt_ref[0, gi], _DN_NT, preferred_element_type=F32)
        y_ref[gi] = y + u * d_ref[0, gi]


def _ssm_mix(uc, layer, w, m, vt, a_step, dtile, bsz):
    ngroups, rows, width = uc.shape
    nchunks = rows // bsz
    assert bsz == SUBLANES and nchunks % SUBLANES == 0
    gb = min(SSM_GROUPS_PER_STEP, ngroups)
    par = lambda shape: pl.BlockSpec((1, gb) + shape, lambda i: (layer, i) + (0,) * len(shape))
    return pl.pallas_call(
        functools.partial(_ssm_kernel, bsz=bsz, nchunks=nchunks),
        grid=(ngroups // gb,),
        in_specs=[pl.BlockSpec((gb, rows, width), lambda i: (i, 0, 0)),
                  par((width, 2 * LANES)), par((width, width)), par((width, 2 * LANES)),
                  par((2, LANES)), par((1, width))],
        out_specs=pl.BlockSpec((gb, rows, width), lambda i: (i, 0, 0)),
        out_shape=jax.ShapeDtypeStruct(uc.shape, F32),
        scratch_shapes=[pltpu.VMEM((gb, 2, rows, LANES), F32),
                        pltpu.VMEM((gb, 2, rows // SUBLANES, LANES), F32)],
        compiler_params=_cparams("parallel"),
        name="ssm_mix",
    )(uc, w, m, vt, a_step, dtile)


def _mixout_kernel(oa_ref, yc_ref, x_ref, mod_ref, wg_ref, bg_ref, an_ref, sn_ref, wo_ref, o_ref, y_scr):
    half = oa_ref.shape[2]
    m = mod_ref[0, 0]
    nsub = y_scr.shape[0]
    rows = x_ref.shape[1] // nsub
    nk = rows // SSM_CHUNK
    gran = LANES // SSM_GROUP
    for sub in range(nsub):
        r = slice(sub * rows, (sub + 1) * rows)
        cols = []
        for tile in range(half // LANES):
            y_t = y_scr.at[sub, tile]
            for octet in range(SSM_CHUNK // gran):
                xs = [yc_ref[tile * gran + g, sub * nk:(sub + 1) * nk, octet * LANES:(octet + 1) * LANES]
                      for g in range(gran)]
                for s, ys in enumerate(_granule_transpose(xs)):
                    y_t[pl.ds(octet * gran + s, nk, stride=CHUNK_PITCH), :] = ys
            cols.append(jnp.concatenate(
                [y_t[kc * CHUNK_PITCH:kc * CHUNK_PITCH + SSM_CHUNK, :] for kc in range(nk)], axis=0))
        y = jax.nn.gelu(jnp.concatenate(cols, axis=1))
        gate = jax.nn.sigmoid(jnp.dot(y.astype(BF16), wg_ref[0], preferred_element_type=F32) + bg_ref[0])
        o_ssm = y * gate
        na = (_rms(oa_ref[0, r, :]) * an_ref[0]).astype(BF16)
        ns = (_rms(o_ssm) * sn_ref[0]).astype(BF16)
        proj = jnp.dot(na, wo_ref[0, 0:half, :], preferred_element_type=F32)
        proj = proj + jnp.dot(ns, wo_ref[0, half:2 * half, :], preferred_element_type=F32)
        o_ref[0, r, :] = x_ref[0, r, :] + m[2:3] * proj


def _mixout(o_attn, y_chunks, x, mod6, layer, w_glu, b_glu, an, sn, w_out):
    bsz, seq, d = x.shape
    half = o_attn.shape[2]
    tm = min(TOKEN_TILE, seq)
    ntile = seq // tm
    nk = tm // SSM_CHUNK
    nsub = max(1, tm // SUB_TILE)
    ngroups = y_chunks.shape[0]
    row = lambda b, i: (b, i, 0)
    return pl.pallas_call(
        _mixout_kernel,
        grid=(bsz, ntile),
        in_specs=[
            pl.BlockSpec((1, tm, half), row),
            pl.BlockSpec((ngroups, nk, CHUNK_WIDTH), lambda b, i: (0, b * ntile + i, 0)),
            pl.BlockSpec((1, tm, d), row),
            pl.BlockSpec((1, 1, N_MOD, d), lambda b, i: (layer, b, 0, 0)),
            _layer_spec((half, half), layer),
            _layer_spec((1, half), layer),
            _layer_spec((1, half), layer),
            _layer_spec((1, half), layer),
            _layer_spec((2 * half, d), layer),
        ],
        out_specs=pl.BlockSpec((1, tm, d), row),
        out_shape=jax.ShapeDtypeStruct(x.shape, F32),
        scratch_shapes=[pltpu.VMEM((nsub, half // LANES, nk // nsub * CHUNK_PITCH, LANES), F32)],
        compiler_params=_cparams("parallel", "parallel"),
        name="mixout",
    )(o_attn, y_chunks, x, mod6, w_glu, b_glu, an, sn, w_out)


def _ffn_kernel(x_ref, mod_ref, n2_ref, wi_ref, wo_ref, fn_ref, o_ref, *, final):
    hidden = wo_ref.shape[1]
    x = x_ref[0]
    m = mod_ref[0, 0]
    h = _rms(x) * n2_ref[0]
    h = h * (1.0 + m[4:5]) + m[3:4]
    gu = jnp.dot(h.astype(BF16), wi_ref[0], preferred_element_type=F32)
    gt = gu[:, 0:hidden]
    act = (gt * jax.nn.sigmoid(gt)) * gu[:, hidden:2 * hidden]
    y = x + m[5:6] * jnp.dot(act.astype(BF16), wo_ref[0], preferred_element_type=F32)
    if final:
        y = _rms(y) * fn_ref[...]
    o_ref[0] = y


def _ffn(x, mod6, layer, n2, w_in, w_out, final_norm, final):
    bsz, seq, d = x.shape
    hidden = w_out.shape[1]
    tm = min(FFN_TOKEN_TILE, seq)
    row = lambda b, i: (b, i, 0)
    return pl.pallas_call(
        functools.partial(_ffn_kernel, final=final),
        grid=(bsz, seq // tm),
        in_specs=[
            pl.BlockSpec((1, tm, d), row),
            pl.BlockSpec((1, 1, N_MOD, d), lambda b, i: (layer, b, 0, 0)),
            _layer_spec((1, d), layer),
            _layer_spec((d, 2 * hidden), layer, single_buffer=True),
            _layer_spec((hidden, d), layer, single_buffer=True),
            pl.BlockSpec((1, d), lambda b, i: (0, 0)),
        ],
        out_specs=pl.BlockSpec((1, tm, d), row),
        out_shape=jax.ShapeDtypeStruct(x.shape, F32),
        compiler_params=_cparams("parallel", "parallel"),
        name="ffn",
    )(x, mod6, n2, w_in, w_out, final_norm)


def _pair_split_perm():
    return np.concatenate([np.arange(0, HEAD_DIM, 2), np.arange(1, HEAD_DIM, 2)])


def _head_pair_order():
    rep = N_Q_HEADS // N_KV_HEADS
    return [h for t in range(rep) for h in (t, t + rep)]


def _rope_tile_columns(base_a, base_b):
    perm = _pair_split_perm()
    ev, od = perm[:HALF_HEAD], perm[HALF_HEAD:]
    return np.concatenate([base_a + ev, base_b + ev, base_a + od, base_b + od])


def _w_in_columns(n_in):
    rep = N_Q_HEADS // N_KV_HEADS
    cols = [_rope_tile_columns(t * HEAD_DIM, (t + rep) * HEAD_DIM) for t in range(rep)]
    k0 = N_Q_HEADS * HEAD_DIM
    cols.append(_rope_tile_columns(k0, k0 + HEAD_DIM))
    cols.append(np.arange(k0 + N_KV_HEADS * HEAD_DIM, n_in))
    return np.concatenate(cols)


def _rope_tables(seq):
    rows = seq // GRID_W
    row_idx = jnp.repeat(jnp.arange(rows, dtype=F32), GRID_W)
    col_idx = jnp.tile(jnp.arange(GRID_W, dtype=F32), rows)
    inv_freq = 1.0 / (ROPE_THETA ** (jnp.arange(0, HALF_HEAD, 2, dtype=F32) / HALF_HEAD))
    ang = jnp.concatenate([row_idx[:, None] * inv_freq, col_idx[:, None] * inv_freq], axis=-1)
    cos, sin = jnp.cos(ang), jnp.sin(ang)
    cos_t = jnp.concatenate([cos, cos, cos, cos], axis=-1)
    sin_t = jnp.concatenate([-sin, -sin, sin, sin], axis=-1)
    return cos_t, sin_t


def kernel(x, c, w_ada, b_ada, norm1, w_in, q_norm, k_norm, ssm_lam_re, ssm_lam_im, ssm_log_dt, ssm_b_re, ssm_b_im, ssm_c_re, ssm_c_im, ssm_d, w_glu, b_glu, attn_out_norm, ssm_out_norm, w_out, norm2, w_ffn_in, w_ffn_out, final_norm):
    bsz, seq, d = x.shape
    depth = w_in.shape[0]
    tm = min(TOKEN_TILE, seq)
    assert seq % tm == 0 and seq % GRID_W == 0 and tm % (SSM_CHUNK * SUBLANES) == 0 and bsz == SUBLANES

    mod6 = _modulation(c, w_ada, b_ada).reshape(depth, bsz, N_MOD, d)
    cos_t, sin_t = _rope_tables(seq)
    head_of_lane = (np.arange(LANES) // HALF_HEAD) % 2
    ones_blk = jnp.asarray(head_of_lane[:, None] == head_of_lane[None, :], dtype=BF16)
    row3 = lambda a: a[:, None, :]

    w_in_arr = w_in.astype(BF16)[:, :, _w_in_columns(w_in.shape[2])]
    gain_cols = _rope_tile_columns(0, 0)
    qg = row3(q_norm[:, gain_cols])
    kg = row3(k_norm[:, gain_cols])
    attn_w = N_Q_HEADS * HEAD_DIM
    head_cols = np.concatenate([h * HEAD_DIM + np.arange(HEAD_DIM) for h in _head_pair_order()])
    out_rows = np.concatenate([head_cols, np.arange(attn_w, w_out.shape[1])])
    an = row3(attn_out_norm[:, head_cols])
    w_out_arr = w_out.astype(BF16)[:, out_rows, :]
    w_glu_b = w_glu.astype(BF16)
    w_ffn_in_b = w_ffn_in.astype(BF16)
    w_ffn_out_b = w_ffn_out.astype(BF16)

    w_s, vt_s, m_s, a_step = _ssm_prep(ssm_lam_re, ssm_lam_im, ssm_log_dt, ssm_b_re, ssm_b_im, ssm_c_re, ssm_c_im)
    ngroups = ssm_lam_re.shape[2]
    dtile = jnp.tile(ssm_d.reshape(depth, ngroups, 1, SSM_GROUP), (1, 1, 1, SSM_CHUNK))

    for i in range(depth):
        q, k, v, u = _premix(x, mod6, i, row3(norm1), w_in_arr, qg, kg, cos_t, sin_t, ones_blk)
        o_attn = _attention(q, k, v)
        y = _ssm_mix(u, i, w_s, m_s, vt_s, a_step, dtile, bsz)
        x = _mixout(o_attn, y, x, mod6, i, w_glu_b, row3(b_glu), an, row3(ssm_out_norm), w_out_arr)
        x = _ffn(x, mod6, i, row3(norm2), w_ffn_in_b, w_ffn_out_b, final_norm.reshape(1, -1), final=(i == depth - 1))
    return x
```

```python
import functools

import numpy as np
import jax
import jax.numpy as jnp
from jax import lax
from jax.experimental import pallas as pl
from jax.experimental.pallas import tpu as pltpu

F32 = jnp.float32
BF16 = jnp.bfloat16

N_Q_HEADS = 8
N_KV_HEADS = 2
HEAD_DIM = 64
HALF_HEAD = HEAD_DIM // 2
GRID_W = 64
ROPE_THETA = 10000.0
SSM_GROUP = 16
SSM_STATE = 64
LAMBDA_RE_MAX = -1e-4
NORM_EPS = 1e-6
N_MOD = 6

LANES = 128
SUBLANES = 8
SSM_CHUNK = 32
CHUNK_BITS = SSM_CHUNK.bit_length() - 1
CHUNK_WIDTH = SSM_CHUNK * SSM_GROUP
CHUNK_PITCH = SSM_CHUNK + SUBLANES
TOKEN_TILE = 1024
SUB_TILE = 512
FFN_TOKEN_TILE = 512
ATTN_Q_BLOCK = 2048
Q_TILE = 64
ATTN_UNROLL = 8
SSM_GROUPS_PER_STEP = 4
PREP_GROUPS_PER_STEP = 8
VMEM_LIMIT = 56 * 1024 * 1024

_DN_NT = (((1,), (1,)), ((), ()))


def _cparams(*sem):
    return pltpu.CompilerParams(dimension_semantics=sem, vmem_limit_bytes=VMEM_LIMIT)


def _rms(x):
    return x * lax.rsqrt(jnp.mean(x * x, axis=-1, keepdims=True) + NORM_EPS)


def _layer_spec(shape, layer, single_buffer=False):
    index_map = lambda b, i: (layer,) + (0,) * len(shape)
    if single_buffer:
        return pl.BlockSpec((1,) + shape, index_map, pipeline_mode=pl.Buffered(1))
    return pl.BlockSpec((1,) + shape, index_map)


def _mod_kernel(c_ref, w_ref, b_ref, o_ref):
    c = c_ref[...]
    ca = (c * jax.nn.sigmoid(c)).astype(BF16)
    o_ref[0] = jnp.dot(ca, w_ref[0].astype(BF16), preferred_element_type=F32) + b_ref[0]


def _modulation(c, w_ada, b_ada):
    depth, d, n = w_ada.shape
    bsz = c.shape[0]
    tn = 1536
    return pl.pallas_call(
        _mod_kernel,
        grid=(depth, n // tn),
        in_specs=[
            pl.BlockSpec((bsz, d), lambda i, j: (0, 0)),
            pl.BlockSpec((1, d, tn), lambda i, j: (i, 0, j)),
            pl.BlockSpec((1, 1, tn), lambda i, j: (i, 0, j)),
        ],
        out_specs=pl.BlockSpec((1, bsz, tn), lambda i, j: (i, 0, j)),
        out_shape=jax.ShapeDtypeStruct((depth, bsz, n), F32),
        compiler_params=_cparams("parallel", "parallel"),
        name="adaln_mod",
    )(c, w_ada, b_ada.reshape(depth, 1, n))


def _granule_transpose(xs):
    lane = lax.broadcasted_iota(jnp.int32, xs[0].shape, 1)
    n = len(xs)
    xs = list(xs)
    for bit in (2, 1, 0):
        d = 1 << bit
        shift = d * SSM_GROUP
        hi = ((lane // SSM_GROUP) & d) != 0
        new = list(xs)
        for a in range(n):
            if a & d:
                continue
            b = a | d
            new[a] = jnp.where(hi, pltpu.roll(xs[b], shift, 1), xs[a])
            new[b] = jnp.where(hi, xs[b], pltpu.roll(xs[a], LANES - shift, 1))
        xs = new
    return xs


def _norm_rope(zt, gain, cos, sin, ones_blk):
    ssq = jnp.dot((zt * zt).astype(BF16), ones_blk, preferred_element_type=F32)
    y = zt * lax.rsqrt(ssq * (1.0 / HEAD_DIM) + NORM_EPS) * gain
    partner = pltpu.roll(y, LANES // 2, 1)
    return y * cos + partner * sin


def _premix_kernel(x_ref, mod_ref, n1_ref, w_ref, qg_ref, kg_ref, cos_ref, sin_ref, ones_ref,
                   q_ref, k_ref, v_ref, uc_ref, u_scr):
    m = mod_ref[0, 0]
    ones_blk = ones_ref[...]
    qg = qg_ref[0] * (HEAD_DIM ** -0.5)
    nq = q_ref.shape[2]
    nsub, ntiles = u_scr.shape[0], u_scr.shape[1]
    rows = x_ref.shape[1] // nsub
    nk = rows // SSM_CHUNK
    gran = LANES // SSM_GROUP
    u0 = nq + 2 * LANES
    for sub in range(nsub):
        r = slice(sub * rows, (sub + 1) * rows)
        h = _rms(x_ref[0, r, :]) * n1_ref[0]
        h = h * (1.0 + m[1:2]) + m[0:1]
        z = jnp.dot(h.astype(BF16), w_ref[0], preferred_element_type=F32)
        cos = cos_ref[r, :]
        sin = sin_ref[r, :]
        for t in range(nq // LANES):
            zt = z[:, t * LANES:(t + 1) * LANES]
            q_ref[0, r, t * LANES:(t + 1) * LANES] = _norm_rope(zt, qg, cos, sin, ones_blk).astype(BF16)
        k_ref[0, r, :] = _norm_rope(z[:, nq:nq + LANES], kg_ref[0], cos, sin, ones_blk).astype(BF16)
        v = z[:, nq + LANES:nq + 2 * LANES]
        v_ref[0, r, 0:LANES] = v.astype(BF16)
        v_ref[0, r, LANES:2 * LANES] = jnp.ones_like(v).astype(BF16)
        for tile in range(ntiles):
            u_t = u_scr.at[sub, tile]
            for kc in range(nk):
                u_t[kc * CHUNK_PITCH:kc * CHUNK_PITCH + SSM_CHUNK, :] = (
                    z[kc * SSM_CHUNK:(kc + 1) * SSM_CHUNK, u0 + tile * LANES:u0 + (tile + 1) * LANES])
            for octet in range(SSM_CHUNK // gran):
                xs = [u_t[pl.ds(octet * gran + s, nk, stride=CHUNK_PITCH), :] for s in range(gran)]
                for g, yg in enumerate(_granule_transpose(xs)):
                    uc_ref[tile * gran + g, sub * nk:(sub + 1) * nk, octet * LANES:(octet + 1) * LANES] = yg


def _premix(x, mod6, layer, n1, w_arr, qg, kg, cos_t, sin_t, ones_blk):
    bsz, seq, d = x.shape
    nc = w_arr.shape[2]
    nq = N_Q_HEADS * HEAD_DIM
    nu = nc - nq - 2 * LANES
    tm = min(TOKEN_TILE, seq)
    ntile = seq // tm
    nk = tm // SSM_CHUNK
    nsub = max(1, tm // SUB_TILE)
    ngroups = nu // SSM_GROUP
    row = lambda b, i: (b, i, 0)
    const = lambda b, i: (0, 0)
    return pl.pallas_call(
        _premix_kernel,
        grid=(bsz, ntile),
        in_specs=[
            pl.BlockSpec((1, tm, d), row),
            pl.BlockSpec((1, 1, N_MOD, d), lambda b, i: (layer, b, 0, 0)),
            _layer_spec((1, d), layer),
            _layer_spec((d, nc), layer),
            _layer_spec((1, LANES), layer),
            _layer_spec((1, LANES), layer),
            pl.BlockSpec((tm, LANES), lambda b, i: (i, 0)),
            pl.BlockSpec((tm, LANES), lambda b, i: (i, 0)),
            pl.BlockSpec((LANES, LANES), const),
        ],
        out_specs=[
            pl.BlockSpec((1, tm, nq), row),
            pl.BlockSpec((1, tm, LANES), row),
            pl.BlockSpec((1, tm, 2 * LANES), row),
            pl.BlockSpec((ngroups, nk, CHUNK_WIDTH), lambda b, i: (0, b * ntile + i, 0)),
        ],
        out_shape=[
            jax.ShapeDtypeStruct((bsz, seq, nq), BF16),
            jax.ShapeDtypeStruct((bsz, seq, LANES), BF16),
            jax.ShapeDtypeStruct((bsz, seq, 2 * LANES), BF16),
            jax.ShapeDtypeStruct((ngroups, bsz * seq // SSM_CHUNK, CHUNK_WIDTH), F32),
        ],
        scratch_shapes=[pltpu.VMEM((nsub, nu // LANES, nk // nsub * CHUNK_PITCH, LANES), F32)],
        compiler_params=_cparams("parallel", "parallel"),
        name="premix",
    )(x, mod6, n1, w_arr, qg, kg, cos_t, sin_t, ones_blk)


def _attn_kernel(q_ref, k_ref, v_ref, o_ref, s_scr, *, tq):
    nsub = q_ref.shape[1] // tq
    rep = N_Q_HEADS // N_KV_HEADS
    lane = lax.broadcasted_iota(jnp.int32, (tq, LANES), 1)
    which = (lane // HALF_HEAD) & 1
    keep_lo = (which == 0).astype(BF16)
    keep_hi = (which == 1).astype(BF16)

    def scores(j, slot):
        r0 = pl.multiple_of(j * tq, tq)
        qs = jnp.concatenate(
            [q_ref[0, pl.ds(r0, tq), (h % rep) * LANES:(h % rep + 1) * LANES] * (keep_lo if h < rep else keep_hi)
             for h in range(N_Q_HEADS)], axis=0)
        s_scr[slot] = lax.dot_general(qs, k_ref[0], _DN_NT, preferred_element_type=F32)

    def finish(j, slot):
        r0 = pl.multiple_of(j * tq, tq)
        s = s_scr[slot]
        e = jnp.exp(s - jnp.max(s, axis=-1, keepdims=True)).astype(BF16)
        pv = jnp.dot(e, v_ref[0], preferred_element_type=F32)
        o = pv[:, 0:LANES] / pv[:, LANES:2 * LANES]
        for t in range(rep):
            tile = jnp.where(lane < HEAD_DIM, o[t * tq:(t + 1) * tq], o[(t + rep) * tq:(t + rep + 1) * tq])
            o_ref[0, pl.ds(r0, tq), t * LANES:(t + 1) * LANES] = tile

    scores(0, 0)

    def run(j0, count, last):
        for t in range(count):
            if not (last and t == count - 1):
                scores(j0 + t + 1, (t + 1) % 2)
            finish(j0 + t, t % 2)

    def body(i, carry):
        run(i * ATTN_UNROLL, ATTN_UNROLL, False)
        return carry

    nloop = nsub // ATTN_UNROLL - 1
    lax.fori_loop(0, nloop, body, 0)
    run(nloop * ATTN_UNROLL, nsub - nloop * ATTN_UNROLL, True)


def _attention(q, k, v):
    bsz, seq, nq = q.shape
    tb = min(ATTN_Q_BLOCK, seq)
    tq = Q_TILE
    assert tb % (ATTN_UNROLL * tq) == 0 and ATTN_UNROLL % 2 == 0
    return pl.pallas_call(
        functools.partial(_attn_kernel, tq=tq),
        grid=(bsz, seq // tb),
        in_specs=[
            pl.BlockSpec((1, tb, nq), lambda b, i: (b, i, 0)),
            pl.BlockSpec((1, seq, LANES), lambda b, i: (b, 0, 0)),
            pl.BlockSpec((1, seq, 2 * LANES), lambda b, i: (b, 0, 0)),
        ],
        out_specs=pl.BlockSpec((1, tb, nq), lambda b, i: (b, i, 0)),
        out_shape=jax.ShapeDtypeStruct((bsz, seq, nq), F32),
        scratch_shapes=[pltpu.VMEM((2, N_Q_HEADS * tq, seq), F32)],
        compiler_params=_cparams("parallel", "parallel"),
        name="attention",
    )(q, k, v)


def _cmul(ar, ai, br, bi):
    return ar * br - ai * bi, ar * bi + ai * br


def _cpow_table(sq_re, sq_im, expo):
    pr = jnp.ones(expo.shape, F32)
    pi = jnp.zeros(expo.shape, F32)
    for bit, (sr, si) in enumerate(zip(sq_re, sq_im)):
        nr, ni = _cmul(pr, pi, sr, si)
        take = (expo & (1 << bit)) != 0
        pr = jnp.where(take, nr, pr)
        pi = jnp.where(take, ni, pi)
    return pr, pi


def _split_bf16(a):
    hi = a.astype(BF16)
    return hi, (a - hi.astype(F32)).astype(BF16)


def _dot_nt_3pass(a, b):
    ah, al = _split_bf16(a)
    bh, bl = _split_bf16(b)
    dg = functools.partial(lax.dot_general, dimension_numbers=_DN_NT, preferred_element_type=F32)
    return dg(ah, bh) + (dg(ah, bl) + dg(al, bh))


def _ssm_prep_kernel(lr_ref, li_ref, ldt_ref, br_ref, bi_ref, cr_ref, ci_ref,
                     w_ref, vt_ref, m_ref, a_ref):
    ngrp = lr_ref.shape[1]
    t = SSM_CHUNK
    shape = (CHUNK_WIDTH, LANES)
    pos = lax.broadcasted_iota(jnp.int32, shape, 0) // SSM_GROUP
    fwd = lax.broadcasted_iota(jnp.int32, shape, 1) < SSM_STATE
    for gi in range(ngrp):
        lr = jnp.minimum(lr_ref[0, gi:gi + 1, :], LAMBDA_RE_MAX)
        li = li_ref[0, gi:gi + 1, :]
        dt = jnp.exp(ldt_ref[0, gi:gi + 1, :])
        mag = jnp.exp(lr * dt)
        a_re = mag * jnp.cos(li * dt)
        a_im = mag * jnp.sin(li * dt)
        den = lr * lr + li * li
        n_re = a_re - 1.0
        k_re = (n_re * lr + a_im * li) / den
        k_im = (a_im * lr - n_re * li) / den
        br = br_ref[0, gi]
        bi = bi_ref[0, gi]
        bt_re = k_re * br - k_im * bi
        bt_im = k_re * bi + k_im * br
        bb_re = jnp.concatenate([bt_re] * t, axis=0)
        bb_im = jnp.concatenate([bt_im] * t, axis=0)
        c_re = jnp.concatenate([cr_ref[0, gi]] * t, axis=0)
        c_im = jnp.concatenate([ci_ref[0, gi]] * t, axis=0)
        sq_re, sq_im = [a_re], [a_im]
        for _ in range(CHUNK_BITS):
            r2, i2 = _cmul(sq_re[-1], sq_im[-1], sq_re[-1], sq_im[-1])
            sq_re.append(r2)
            sq_im.append(i2)
        a_ref[0, gi, 0:1, :] = sq_re[CHUNK_BITS]
        a_ref[0, gi, 1:2, :] = sq_im[CHUNK_BITS]
        up_re, up_im = _cpow_table(sq_re[:CHUNK_BITS], sq_im[:CHUNK_BITS], pos)
        flip = lambda p: jnp.concatenate(
            [p[(t - 1 - j) * SSM_GROUP:(t - j) * SSM_GROUP] for j in range(t)], axis=0)
        dn_re, dn_im = flip(up_re), flip(up_im)
        wr, wi = _cmul(jnp.where(fwd, dn_re, up_re), jnp.where(fwd, dn_im, up_im), bb_re, bb_im)
        w_ref[0, gi, :, 0:LANES] = wr.astype(BF16)
        w_ref[0, gi, :, LANES:2 * LANES] = wi.astype(BF16)
        pr, pi = _cmul(a_re, a_im, jnp.where(fwd, up_re, dn_re), jnp.where(fwd, up_im, dn_im))
        vr, vi = _cmul(pr, pi, c_re, c_im)
        vt_ref[0, gi, :, 0:LANES] = vr.astype(BF16)
        vt_ref[0, gi, :, LANES:2 * LANES] = (-vi).astype(BF16)
        zero = jnp.zeros(shape, F32)
        left_re = jnp.where(fwd, zero, vr)
        left_im = jnp.where(fwd, zero, vi)
        qr, qi = _cmul(up_re, up_im, c_re, c_im)
        both = jnp.where(fwd, 0, pos) == 0
        right_re = jnp.where(both, qr, zero)
        right_im = jnp.where(both, qi, zero)
        strip = jnp.concatenate(
            [_dot_nt_3pass(bt_re, left_re) - _dot_nt_3pass(bt_im, left_im),
             _dot_nt_3pass(bt_re, right_re) - _dot_nt_3pass(bt_im, right_im)], axis=1)
        for s in range(t):
            start = (t - s) * SSM_GROUP
            win = pltpu.roll(strip, 2 * CHUNK_WIDTH - start, 1)[:, 0:CHUNK_WIDTH]
            m_ref[0, gi, s * SSM_GROUP:(s + 1) * SSM_GROUP, :] = win.astype(BF16)


def _ssm_prep(lam_re, lam_im, log_dt, b_re, b_im, c_re, c_im):
    depth, _, ngroups, nstate = lam_re.shape
    cat = lambda a: jnp.concatenate([a[:, 0], a[:, 1]], axis=-1)
    lr = cat(lam_re)
    li = cat(lam_im)
    ldt = cat(jnp.broadcast_to(log_dt[..., None], lam_re.shape))
    bt_re = cat(jnp.swapaxes(b_re, -1, -2))
    bt_im = cat(jnp.swapaxes(b_im, -1, -2))
    cc_re = cat(c_re)
    cc_im = cat(c_im)
    gb = min(PREP_GROUPS_PER_STEP, ngroups)
    vec = pl.BlockSpec((1, gb, 2 * nstate), lambda i, j: (i, j, 0))
    mat = pl.BlockSpec((1, gb, SSM_GROUP, 2 * nstate), lambda i, j: (i, j, 0, 0))
    big = lambda n: pl.BlockSpec((1, gb, CHUNK_WIDTH, n), lambda i, j: (i, j, 0, 0))
    return pl.pallas_call(
        _ssm_prep_kernel,
        grid=(depth, ngroups // gb),
        in_specs=[vec, vec, vec, mat, mat, mat, mat],
        out_specs=[big(2 * LANES), big(2 * LANES), big(CHUNK_WIDTH),
                   pl.BlockSpec((1, gb, 2, 2 * nstate), lambda i, j: (i, j, 0, 0))],
        out_shape=[
            jax.ShapeDtypeStruct((depth, ngroups, CHUNK_WIDTH, 2 * LANES), BF16),
            jax.ShapeDtypeStruct((depth, ngroups, CHUNK_WIDTH, 2 * LANES), BF16),
            jax.ShapeDtypeStruct((depth, ngroups, CHUNK_WIDTH, CHUNK_WIDTH), BF16),
            jax.ShapeDtypeStruct((depth, ngroups, 2, 2 * nstate), F32),
        ],
        compiler_params=_cparams("parallel", "parallel"),
        name="ssm_prep",
    )(lr, li, ldt, bt_re, bt_im, cc_re, cc_im)


def _ssm_kernel(u_ref, w_ref, m_ref, vt_ref, a_ref, d_ref, y_ref, s_scr, p_scr, *, bsz, nchunks):
    ngrp = u_ref.shape[0]
    rows = u_ref.shape[1]
    nv = nchunks // SUBLANES
    ntile = rows // SUBLANES
    tiled = (ntile, SUBLANES, LANES)
    sub = lax.broadcasted_iota(jnp.int32, tiled, 1)
    fwd = lax.broadcasted_iota(jnp.int32, tiled, 2) < SSM_STATE
    depth_in_tile = jnp.where(fwd, sub, SUBLANES - 1 - sub)
    fwd_t = lax.broadcasted_iota(jnp.int32, (ntile, LANES), 1) < SSM_STATE
    sub8 = lax.broadcasted_iota(jnp.int32, (SUBLANES, LANES), 0)
    fwd8 = lax.broadcasted_iota(jnp.int32, (SUBLANES, LANES), 1) < SSM_STATE

    def shifted(x, d):
        ok = depth_in_tile >= d
        moved = jnp.where(fwd, pltpu.roll(x, d, 1), pltpu.roll(x, SUBLANES - d, 1))
        return jnp.where(ok, moved, jnp.zeros_like(x))

    step8, pw = [], []
    for gi in range(ngrp):
        sq_re, sq_im = [a_ref[0, gi, 0:1, :]], [a_ref[0, gi, 1:2, :]]
        for _ in range(3):
            r2, i2 = _cmul(sq_re[-1], sq_im[-1], sq_re[-1], sq_im[-1])
            sq_re.append(r2)
            sq_im.append(i2)
        step8.append((sq_re[3], sq_im[3]))
        pw.append(_cpow_table(sq_re[:3], sq_im[:3], jnp.where(fwd8, sub8, SUBLANES - 1 - sub8)))

        s = jnp.dot(u_ref[gi].astype(BF16), w_ref[0, gi], preferred_element_type=F32)
        x_re, x_im = s[:, 0:LANES].reshape(tiled), s[:, LANES:2 * LANES].reshape(tiled)
        for bit in range(3):
            mr, mi = _cmul(sq_re[bit], sq_im[bit], shifted(x_re, 1 << bit), shifted(x_im, 1 << bit))
            x_re, x_im = x_re + mr, x_im + mi
        for part, x in enumerate((x_re, x_im)):
            s_p = s_scr.at[gi, part]
            p_p = p_scr.at[gi, part]
            s_p[...] = x.reshape(rows, LANES)
            edge = jnp.where(fwd_t, s_p[pl.ds(SUBLANES - 1, ntile, stride=SUBLANES), :],
                             s_p[pl.ds(0, ntile, stride=SUBLANES), :])
            for b in range(bsz):
                p_p[pl.ds(b, nv, stride=bsz), :] = edge[b * nv:(b + 1) * nv]
            s_p[...] = shifted(x, 1).reshape(rows, LANES)

    def tile_step(i, carry):
        out = []
        for gi in range(ngrp):
            c = carry[gi]
            rf = [pl.multiple_of((b * nv + i) * SUBLANES, SUBLANES) for b in range(bsz)]
            rb = [pl.multiple_of((b * nv + nv - 1 - i) * SUBLANES, SUBLANES) for b in range(bsz)]
            e = [[jnp.where(fwd8, s_scr[gi, part, pl.ds(rf[b], SUBLANES), :],
                            s_scr[gi, part, pl.ds(rb[b], SUBLANES), :]) for part in range(2)]
                 for b in range(bsz)]
            for b in range(bsz):
                cb = [jnp.broadcast_to(cp[b:b + 1, :], (SUBLANES, LANES)) for cp in c]
                h = _cmul(pw[gi][0], pw[gi][1], cb[0], cb[1])
                for part in range(2):
                    hp = h[part] + e[b][part]
                    s_scr[gi, part, pl.ds(rf[b], SUBLANES), 0:SSM_STATE] = hp[:, 0:SSM_STATE]
                    s_scr[gi, part, pl.ds(rb[b], SUBLANES), SSM_STATE:LANES] = hp[:, SSM_STATE:LANES]
            pf = pl.multiple_of(i * bsz, bsz)
            pb = pl.multiple_of((nv - 1 - i) * bsz, bsz)
            n = _cmul(step8[gi][0], step8[gi][1], c[0], c[1])
            out.append(tuple(
                n[part] + jnp.where(fwd8, p_scr[gi, part, pl.ds(pf, bsz), :], p_scr[gi, part, pl.ds(pb, bsz), :])
                for part in range(2)))
        return tuple(out)

    zero = jnp.zeros((bsz, LANES), F32)
    lax.fori_loop(0, nv, tile_step, tuple((zero, zero) for _ in range(ngrp)))

    for gi in range(ngrp):
        u = u_ref[gi]
        y = jnp.dot(u.astype(BF16), m_ref[0, gi], preferred_element_type=F32)
        h = jnp.concatenate([s_scr[gi, 0], s_scr[gi, 1]], axis=1).astype(BF16)
        y = y + lax.dot_general(h, vt_ref[0, gi], _DN_NT, preferred_element_type=F32)
        y_ref[gi] = y + u * d_ref[0, gi]


def _ssm_mix(uc, layer, w, m, vt, a_step, dtile, bsz):
    ngroups, rows, width = uc.shape
    nchunks = rows // bsz
    assert bsz == SUBLANES and nchunks % SUBLANES == 0
    gb = min(SSM_GROUPS_PER_STEP, ngroups)
    par = lambda shape: pl.BlockSpec((1, gb) + shape, lambda i: (layer, i) + (0,) * len(shape))
    return pl.pallas_call(
        functools.partial(_ssm_kernel, bsz=bsz, nchunks=nchunks),
        grid=(ngroups // gb,),
        in_specs=[pl.BlockSpec((gb, rows, width), lambda i: (i, 0, 0)),
                  par((width, 2 * LANES)), par((width, width)), par((width, 2 * LANES)),
                  par((2, LANES)), par((1, width))],
        out_specs=pl.BlockSpec((gb, rows, width), lambda i: (i, 0, 0)),
        out_shape=jax.ShapeDtypeStruct(uc.shape, F32),
        scratch_shapes=[pltpu.VMEM((gb, 2, rows, LANES), F32),
                        pltpu.VMEM((gb, 2, rows // SUBLANES, LANES), F32)],
        compiler_params=_cparams("parallel"),
        name="ssm_mix",
    )(uc, w, m, vt, a_step, dtile)


def _mixout_kernel(oa_ref, yc_ref, x_ref, mod_ref, wg_ref, bg_ref, an_ref, sn_ref, wo_ref, o_ref, y_scr):
    half = oa_ref.shape[2]
    m = mod_ref[0, 0]
    nsub = y_scr.shape[0]
    rows = x_ref.shape[1] // nsub
    nk = rows // SSM_CHUNK
    gran = LANES // SSM_GROUP
    for sub in range(nsub):
        r = slice(sub * rows, (sub + 1) * rows)
        cols = []
        for tile in range(half // LANES):
            y_t = y_scr.at[sub, tile]
            for octet in range(SSM_CHUNK // gran):
                xs = [yc_ref[tile * gran + g, sub * nk:(sub + 1) * nk, octet * LANES:(octet + 1) * LANES]
                      for g in range(gran)]
                for s, ys in enumerate(_granule_transpose(xs)):
                    y_t[pl.ds(octet * gran + s, nk, stride=CHUNK_PITCH), :] = ys
            cols.append(jnp.concatenate(
                [y_t[kc * CHUNK_PITCH:kc * CHUNK_PITCH + SSM_CHUNK, :] for kc in range(nk)], axis=0))
        y = jax.nn.gelu(jnp.concatenate(cols, axis=1))
        gate = jax.nn.sigmoid(jnp.dot(y.astype(BF16), wg_ref[0], preferred_element_type=F32) + bg_ref[0])
        o_ssm = y * gate
        na = (_rms(oa_ref[0, r, :]) * an_ref[0]).astype(BF16)
        ns = (_rms(o_ssm) * sn_ref[0]).astype(BF16)
        proj = jnp.dot(na, wo_ref[0, 0:half, :], preferred_element_type=F32)
        proj = proj + jnp.dot(ns, wo_ref[0, half:2 * half, :], preferred_element_type=F32)
        o_ref[0, r, :] = x_ref[0, r, :] + m[2:3] * proj


def _mixout(o_attn, y_chunks, x, mod6, layer, w_glu, b_glu, an, sn, w_out):
    bsz, seq, d = x.shape
    half = o_attn.shape[2]
    tm = min(TOKEN_TILE, seq)
    ntile = seq // tm
    nk = tm // SSM_CHUNK
    nsub = max(1, tm // SUB_TILE)
    ngroups = y_chunks.shape[0]
    row = lambda b, i: (b, i, 0)
    return pl.pallas_call(
        _mixout_kernel,
        grid=(bsz, ntile),
        in_specs=[
            pl.BlockSpec((1, tm, half), row),
            pl.BlockSpec((ngroups, nk, CHUNK_WIDTH), lambda b, i: (0, b * ntile + i, 0)),
            pl.BlockSpec((1, tm, d), row),
            pl.BlockSpec((1, 1, N_MOD, d), lambda b, i: (layer, b, 0, 0)),
            _layer_spec((half, half), layer),
            _layer_spec((1, half), layer),
            _layer_spec((1, half), layer),
            _layer_spec((1, half), layer),
            _layer_spec((2 * half, d), layer),
        ],
        out_specs=pl.BlockSpec((1, tm, d), row),
        out_shape=jax.ShapeDtypeStruct(x.shape, F32),
        scratch_shapes=[pltpu.VMEM((nsub, half // LANES, nk // nsub * CHUNK_PITCH, LANES), F32)],
        compiler_params=_cparams("parallel", "parallel"),
        name="mixout",
    )(o_attn, y_chunks, x, mod6, w_glu, b_glu, an, sn, w_out)


def _ffn_kernel(x_ref, mod_ref, n2_ref, wi_ref, wo_ref, fn_ref, o_ref, *, final):
    hidden = wo_ref.shape[1]
    x = x_ref[0]
    m = mod_ref[0, 0]
    h = _rms(x) * n2_ref[0]
    h = h * (1.0 + m[4:5]) + m[3:4]
    gu = jnp.dot(h.astype(BF16), wi_ref[0], preferred_element_type=F32)
    gt = gu[:, 0:hidden]
    act = (gt * jax.nn.sigmoid(gt)) * gu[:, hidden:2 * hidden]
    y = x + m[5:6] * jnp.dot(act.astype(BF16), wo_ref[0], preferred_element_type=F32)
    if final:
        y = _rms(y) * fn_ref[...]
    o_ref[0] = y


def _ffn(x, mod6, layer, n2, w_in, w_out, final_norm, final):
    bsz, seq, d = x.shape
    hidden = w_out.shape[1]
    tm = min(FFN_TOKEN_TILE, seq)
    row = lambda b, i: (b, i, 0)
    return pl.pallas_call(
        functools.partial(_ffn_kernel, final=final),
        grid=(bsz, seq // tm),
        in_specs=[
            pl.BlockSpec((1, tm, d), row),
            pl.BlockSpec((1, 1, N_MOD, d), lambda b, i: (layer, b, 0, 0)),
            _layer_spec((1, d), layer),
            _layer_spec((d, 2 * hidden), layer, single_buffer=True),
            _layer_spec((hidden, d), layer, single_buffer=True),
            pl.BlockSpec((1, d), lambda b, i: (0, 0)),
        ],
        out_specs=pl.BlockSpec((1, tm, d), row),
        out_shape=jax.ShapeDtypeStruct(x.shape, F32),
        compiler_params=_cparams("parallel", "parallel"),
        name="ffn",
    )(x, mod6, n2, w_in, w_out, final_norm)


def _pair_split_perm():
    return np.concatenate([np.arange(0, HEAD_DIM, 2), np.arange(1, HEAD_DIM, 2)])


def _head_pair_order():
    rep = N_Q_HEADS // N_KV_HEADS
    return [h for t in range(rep) for h in (t, t + rep)]


def _rope_tile_columns(base_a, base_b):
    perm = _pair_split_perm()
    ev, od = perm[:HALF_HEAD], perm[HALF_HEAD:]
    return np.concatenate([base_a + ev, base_b + ev, base_a + od, base_b + od])


def _w_in_columns(n_in):
    rep = N_Q_HEADS // N_KV_HEADS
    cols = [_rope_tile_columns(t * HEAD_DIM, (t + rep) * HEAD_DIM) for t in range(rep)]
    k0 = N_Q_HEADS * HEAD_DIM
    cols.append(_rope_tile_columns(k0, k0 + HEAD_DIM))
    cols.append(np.arange(k0 + N_KV_HEADS * HEAD_DIM, n_in))
    return np.concatenate(cols)


def _rope_tables(seq):
    rows = seq // GRID_W
    row_idx = jnp.repeat(jnp.arange(rows, dtype=F32), GRID_W)
    col_idx = jnp.tile(jnp.arange(GRID_W, dtype=F32), rows)
    inv_freq = 1.0 / (ROPE_THETA ** (jnp.arange(0, HALF_HEAD, 2, dtype=F32) / HALF_HEAD))
    ang = jnp.concatenate([row_idx[:, None] * inv_freq, col_idx[:, None] * inv_freq], axis=-1)
    cos, sin = jnp.cos(ang), jnp.sin(ang)
    cos_t = jnp.concatenate([cos, cos, cos, cos], axis=-1)
    sin_t = jnp.concatenate([-sin, -sin, sin, sin], axis=-1)
    return cos_t, sin_t


def kernel(x, c, w_ada, b_ada, norm1, w_in, q_norm, k_norm, ssm_lam_re, ssm_lam_im, ssm_log_dt, ssm_b_re, ssm_b_im, ssm_c_re, ssm_c_im, ssm_d, w_glu, b_glu, attn_out_norm, ssm_out_norm, w_out, norm2, w_ffn_in, w_ffn_out, final_norm):
    bsz, seq, d = x.shape
    depth = w_in.shape[0]
    tm = min(TOKEN_TILE, seq)
    assert seq % tm == 0 and seq % GRID_W == 0 and tm % (SSM_CHUNK * SUBLANES) == 0 and bsz == SUBLANES

    mod6 = _modulation(c, w_ada, b_ada).reshape(depth, bsz, N_MOD, d)
    cos_t, sin_t = _rope_tables(seq)
    head_of_lane = (np.arange(LANES) // HALF_HEAD) % 2
    ones_blk = jnp.asarray(head_of_lane[:, None] == head_of_lane[None, :], dtype=BF16)
    row3 = lambda a: a[:, None, :]

    w_in_arr = w_in.astype(BF16)[:, :, _w_in_columns(w_in.shape[2])]
    gain_cols = _rope_tile_columns(0, 0)
    qg = row3(q_norm[:, gain_cols])
    kg = row3(k_norm[:, gain_cols])
    attn_w = N_Q_HEADS * HEAD_DIM
    head_cols = np.concatenate([h * HEAD_DIM + np.arange(HEAD_DIM) for h in _head_pair_order()])
    out_rows = np.concatenate([head_cols, np.arange(attn_w, w_out.shape[1])])
    an = row3(attn_out_norm[:, head_cols])
    w_out_arr = w_out.astype(BF16)[:, out_rows, :]
    w_glu_b = w_glu.astype(BF16)
    w_ffn_in_b = w_ffn_in.astype(BF16)
    w_ffn_out_b = w_ffn_out.astype(BF16)

    w_s, vt_s, m_s, a_step = _ssm_prep(ssm_lam_re, ssm_lam_im, ssm_log_dt, ssm_b_re, ssm_b_im, ssm_c_re, ssm_c_im)
    ngroups = ssm_lam_re.shape[2]
    dtile = jnp.tile(ssm_d.reshape(depth, ngroups, 1, SSM_GROUP), (1, 1, 1, SSM_CHUNK))

    for i in range(depth):
        q, k, v, u = _premix(x, mod6, i, row3(norm1), w_in_arr, qg, kg, cos_t, sin_t, ones_blk)
        o_attn = _attention(q, k, v)
        y = _ssm_mix(u, i, w_s, m_s, vt_s, a_step, dtile, bsz)
        x = _mixout(o_attn, y, x, mod6, i, w_glu_b, row3(b_glu), an, row3(ssm_out_norm), w_out_arr)
        x = _ffn(x, mod6, i, row3(norm2), w_ffn_in_b, w_ffn_out_b, final_norm.reshape(1, -1), final=(i == depth - 1))
    return x
```
